```python
import jax, jax.numpy as jnp
from jax import lax
import numpy as np

D_MODEL = 4096
BATCH = 2
SEQ = 4096
DEPTH = 2

HEAD_DIM = 128
DSWA_HEADS = D_MODEL // (2 * HEAD_DIM)
DSWA_PATTERNS = ((128, 1), (512, 4), (2048, 16))
DSWA_BAND = 128
GDN_HEADS = D_MODEL // (2 * HEAD_DIM)
GDN_DK = 128
GDN_DV = 128
GDN_CONV = 4
GDN_CHUNK = 64
FOX_HEADS = D_MODEL // HEAD_DIM
FOX_BLOCK = 128
ROPE_THETA = 500000.0
ROPE_DIM = HEAD_DIM // 4
N_EXPERTS = 32
N_GROUPS = 8
EXPERTS_PER_GROUP = N_EXPERTS // N_GROUPS
TOP_K = 2
D_EXPERT = 768
MOE_BLOCK = 128
LN_EPS = 1e-5
NORM_EPS = 1e-6
NEG_INF = -1e30
DEEPNORM_ALPHA = (2 * DEPTH) ** 0.25
DEEPNORM_BETA = (8 * DEPTH) ** -0.25
N_EVEN = (DEPTH + 1) // 2
N_ODD = DEPTH // 2
DSWA_W = DSWA_HEADS * HEAD_DIM
GDN_QKV = GDN_HEADS * (2 * GDN_DK + GDN_DV)
GDN_W = GDN_HEADS * GDN_DV
IN_EVEN = 3 * DSWA_W + GDN_QKV + GDN_W + 2 * GDN_HEADS
FOX_W = FOX_HEADS * HEAD_DIM
IN_ODD = 3 * FOX_W + FOX_HEADS

kernel_name = 'hybrid_dswa_gdn_fox_moe'


def layer_norm(x, g, b):
    xf = x.astype(jnp.float32)
    mu = xf.mean(-1, keepdims=True)
    var = jnp.square(xf - mu).mean(-1, keepdims=True)
    return ((xf - mu) * lax.rsqrt(var + LN_EPS) * g + b).astype(x.dtype)


def l2_normalize(t):
    t = t.astype(jnp.float32)
    return t * lax.rsqrt(jnp.sum(t * t, axis=-1, keepdims=True) + NORM_EPS)


def split_heads(t, n_heads):
    b, s, _ = t.shape
    return t.reshape(b, s, n_heads, -1).transpose(0, 2, 1, 3)


def merge_heads(t):
    b, h, s, d = t.shape
    return t.transpose(0, 2, 1, 3).reshape(b, s, h * d)


def partial_rotary(t, positions):
    half = ROPE_DIM // 2
    inv_freq = ROPE_THETA ** (-jnp.arange(half, dtype=jnp.float32) / half)
    ang = positions.astype(jnp.float32)[:, None] * inv_freq[None, :]
    cos, sin = jnp.cos(ang), jnp.sin(ang)
    rot = t[..., :ROPE_DIM].astype(jnp.float32)
    r1, r2 = rot[..., :half], rot[..., half:]
    rot = jnp.concatenate([r1 * cos - r2 * sin, r2 * cos + r1 * sin], axis=-1)
    return jnp.concatenate([rot.astype(t.dtype), t[..., ROPE_DIM:]], axis=-1)


def dilated_window_attention(q, k, v, dilation, window):
    b, h, s, d = q.shape
    n_back = window // dilation
    assert n_back <= DSWA_BAND
    sub_len = s // dilation
    n_blk = -(-sub_len // DSWA_BAND)
    pad = n_blk * DSWA_BAND - sub_len

    def regroup(t):
        t = t.reshape(b, h, sub_len, dilation, d).transpose(0, 1, 3, 2, 4)
        t = jnp.pad(t, ((0, 0), (0, 0), (0, 0), (0, pad), (0, 0)))
        return t.reshape(b, h, dilation, n_blk, DSWA_BAND, d)

    def with_prev(t):
        prev = jnp.pad(t[:, :, :, :-1], ((0, 0), (0, 0), (0, 0), (1, 0), (0, 0), (0, 0)))
        return jnp.concatenate([prev, t], axis=4)

    qb = regroup(q)
    kc = with_prev(regroup(k))
    vc = with_prev(regroup(v))
    scores = jnp.einsum('bhrnqd,bhrnkd->bhrnqk', qb, kc,
                        preferred_element_type=jnp.float32) * (d ** -0.5)
    qi = jnp.arange(DSWA_BAND)[:, None]
    kj = jnp.arange(2 * DSWA_BAND)[None, :]
    dist = qi + DSWA_BAND - kj
    key_idx = jnp.arange(n_blk)[:, None, None] * DSWA_BAND - DSWA_BAND + kj
    mask = (dist >= 0) & (dist <= n_back) & (key_idx >= 0)
    scores = jnp.where(mask, scores, NEG_INF)
    m = scores.max(-1, keepdims=True)
    p = jnp.exp(scores - m)
    den = p.sum(-1, keepdims=True)
    o = jnp.einsum('bhrnqk,bhrnkd->bhrnqd', p, vc.astype(jnp.float32)) / den
    lse = (m + jnp.log(den))[..., 0]

    def ungroup(t):
        t = t.reshape(t.shape[:3] + (n_blk * DSWA_BAND,) + t.shape[5:])[:, :, :, :sub_len]
        t = jnp.swapaxes(t, 2, 3)
        return t.reshape((b, h, s) + t.shape[4:])

    return ungroup(o), ungroup(lse)


def dilated_mixture_attention(q, k, v):
    outs, lses = [], []
    for window, dilation in DSWA_PATTERNS:
        o, lse = dilated_window_attention(q, k, v, dilation, window)
        outs.append(o)
        lses.append(lse)
    wgt = jax.nn.softmax(jnp.stack(lses), axis=0)[..., None]
    return (wgt * jnp.stack(outs)).sum(0)


def causal_depthwise_conv(x, w):
    return lax.conv_general_dilated(
        x, w[:, None, :], window_strides=(1,), padding=[(GDN_CONV - 1, 0)],
        dimension_numbers=('NWC', 'WIO', 'NWC'), feature_group_count=x.shape[-1])


def gated_delta_rule(q, k, v, g, beta):
    f32 = jnp.float32
    b, h, s, dk = q.shape
    dv = v.shape[-1]
    c = GDN_CHUNK
    n = s // c
    q, k, v = [t.astype(f32).reshape(b, h, n, c, t.shape[-1]) for t in (q, k, v)]
    g = g.astype(f32).reshape(b, h, n, c)
    beta = beta.astype(f32).reshape(b, h, n, c)
    gc = jnp.cumsum(g, axis=-1)
    causal = jnp.tril(jnp.ones((c, c), dtype=bool))
    decay = jnp.exp(jnp.where(causal, gc[..., :, None] - gc[..., None, :], -jnp.inf))
    kb = k * beta[..., None]
    strict = jnp.tril(jnp.ones((c, c), f32), -1)
    a_mat = jnp.einsum('bhnid,bhnjd->bhnij', kb, k) * decay * strict + jnp.eye(c, dtype=f32)
    rhs = jnp.concatenate([v * beta[..., None], kb * jnp.exp(gc)[..., None]], axis=-1)
    sol = lax.linalg.triangular_solve(a_mat, rhs, left_side=True, lower=True, unit_diagonal=True)
    u, w = sol[..., :dv], sol[..., dv:]
    attn = jnp.einsum('bhnid,bhnjd->bhnij', q, k) * decay
    q_dec = q * jnp.exp(gc)[..., None]
    k_dec = k * jnp.exp(gc[..., -1:] - gc)[..., None]
    last = jnp.exp(gc[..., -1])

    def step(state, xs):
        u_c, w_c, q_c, k_c, a_c, l_c = xs
        v_new = u_c - jnp.einsum('bhck,bhkv->bhcv', w_c, state)
        o_c = jnp.einsum('bhck,bhkv->bhcv', q_c, state) + jnp.einsum('bhij,bhjv->bhiv', a_c, v_new)
        state = state * l_c[..., None, None] + jnp.einsum('bhck,bhcv->bhkv', k_c, v_new)
        return state, o_c

    xs = tuple(jnp.moveaxis(t, 2, 0) for t in (u, w, q_dec, k_dec, attn, last))
    _, o = lax.scan(step, jnp.zeros((b, h, dk, dv), f32), xs)
    return jnp.moveaxis(o, 0, 2).reshape(b, h, s, dv)


def forgetting_attention(q, k, v, log_f):
    b, h, s, d = q.shape
    cum = jnp.cumsum(log_f, axis=-1)
    pos = jnp.arange(s)
    outs = []
    for blk in range(s // FOX_BLOCK):
        q0, q1 = blk * FOX_BLOCK, (blk + 1) * FOX_BLOCK
        logits = jnp.einsum('bhqd,bhkd->bhqk', q[:, :, q0:q1], k[:, :, :q1],
                            preferred_element_type=jnp.float32) * (d ** -0.5)
        logits = logits + cum[:, :, q0:q1, None] - cum[:, :, None, :q1]
        logits = jnp.where(pos[q0:q1, None] >= pos[None, :q1], logits, NEG_INF)
        p = jax.nn.softmax(logits, axis=-1)
        outs.append(jnp.einsum('bhqk,bhkd->bhqd', p.astype(v.dtype), v[:, :, :q1]))
    return jnp.concatenate(outs, axis=2)


def even_mixer(x, w_in, conv_w, a_log, dt_bias, gdn_norm_w, w_out, positions):
    f32 = jnp.float32
    b, s, _ = x.shape
    h = jnp.einsum('bsd,de->bse', x, w_in)
    cuts = [DSWA_W, 2 * DSWA_W, 3 * DSWA_W, 3 * DSWA_W + GDN_QKV,
            3 * DSWA_W + GDN_QKV + GDN_W, 3 * DSWA_W + GDN_QKV + GDN_W + GDN_HEADS]
    qa, ka, va, qkv_b, gate_b, beta_logit, a_logit = jnp.split(h, cuts, axis=-1)
    qa = partial_rotary(split_heads(qa, DSWA_HEADS), positions)
    ka = partial_rotary(split_heads(ka, DSWA_HEADS), positions)
    o_a = merge_heads(dilated_mixture_attention(qa, ka, split_heads(va, DSWA_HEADS))).astype(x.dtype)
    qkv_b = jax.nn.silu(causal_depthwise_conv(qkv_b, conv_w))
    qb, kb, vb = jnp.split(qkv_b, [GDN_HEADS * GDN_DK, 2 * GDN_HEADS * GDN_DK], axis=-1)
    qb = l2_normalize(split_heads(qb, GDN_HEADS)) * (GDN_DK ** -0.5)
    kb = l2_normalize(split_heads(kb, GDN_HEADS))
    vb = split_heads(vb, GDN_HEADS)
    g = -jnp.exp(a_log.astype(f32)) * jax.nn.softplus(a_logit.astype(f32) + dt_bias.astype(f32))
    beta = jax.nn.sigmoid(beta_logit.astype(f32))
    o_b = gated_delta_rule(qb, kb, vb, g.transpose(0, 2, 1), beta.transpose(0, 2, 1))
    o_b = o_b.transpose(0, 2, 1, 3)
    o_b = o_b * lax.rsqrt(jnp.mean(o_b * o_b, axis=-1, keepdims=True) + NORM_EPS) * gdn_norm_w
    o_b = o_b * jax.nn.silu(gate_b.astype(f32).reshape(b, s, GDN_HEADS, GDN_DV))
    o_b = o_b.reshape(b, s, GDN_W).astype(x.dtype)
    return jnp.einsum('bse,ed->bsd', jnp.concatenate([o_a, o_b], axis=-1), w_out)


def odd_mixer(x, w_in, b_f, w_out):
    h = jnp.einsum('bsd,de->bse', x, w_in)
    qc, kc, vc, f_logit = jnp.split(h, [FOX_W, 2 * FOX_W, 3 * FOX_W], axis=-1)
    log_f = jax.nn.log_sigmoid((f_logit + b_f).astype(jnp.float32)).transpose(0, 2, 1)
    o = forgetting_attention(split_heads(qc, FOX_HEADS), split_heads(kc, FOX_HEADS),
                             split_heads(vc, FOX_HEADS), log_f)
    return jnp.einsum('bse,ed->bsd', merge_heads(o).astype(x.dtype), w_out)


def moe_ffn(x, router_w, router_bias, w_gate, w_up, w_down):
    f32 = jnp.float32
    b, s, d = x.shape
    xt = x.reshape(-1, d)
    n_tok = xt.shape[0]
    scores = jax.nn.sigmoid(jnp.einsum('td,de->te', xt, router_w, preferred_element_type=f32))
    sel = (scores + router_bias.astype(f32)).reshape(n_tok, N_GROUPS, EXPERTS_PER_GROUP)
    group_score = lax.top_k(sel, TOP_K)[0].sum(-1)
    group = jnp.argmax(group_score, axis=-1)
    in_group = jnp.take_along_axis(sel, group[:, None, None], axis=1)[:, 0]
    _, local = lax.top_k(in_group, TOP_K)
    expert = group[:, None] * EXPERTS_PER_GROUP + local
    gate = jnp.take_along_axis(scores, expert, axis=1)
    gate = gate / gate.sum(-1, keepdims=True)
    n_assign = n_tok * TOP_K
    flat_e = expert.reshape(-1)
    flat_tok = jnp.repeat(jnp.arange(n_tok, dtype=jnp.int32), TOP_K)
    flat_gate = gate.reshape(-1)
    order = jnp.argsort(flat_e)
    e_sorted = flat_e[order]
    counts = jnp.bincount(flat_e, length=N_EXPERTS)
    padded = (counts + MOE_BLOCK - 1) // MOE_BLOCK * MOE_BLOCK
    pad_end = jnp.cumsum(padded)
    pad_start = pad_end - padded
    start = jnp.cumsum(counts) - counts
    dest = pad_start[e_sorted] + jnp.arange(n_assign) - start[e_sorted]
    n_blocks = -(-(n_assign + N_EXPERTS * (MOE_BLOCK - 1)) // MOE_BLOCK)
    buf_len = n_blocks * MOE_BLOCK
    buf_tok = jnp.zeros((buf_len,), jnp.int32).at[dest].set(flat_tok[order])
    buf_gate = jnp.zeros((buf_len,), f32).at[dest].set(flat_gate[order])
    block_expert = jnp.minimum(
        jnp.searchsorted(pad_end, jnp.arange(n_blocks) * MOE_BLOCK, side='right'), N_EXPERTS - 1)
    xb = xt[buf_tok].reshape(n_blocks, MOE_BLOCK, d)

    def expert_block(args):
        xs, e = args
        hid = jax.nn.silu(xs @ w_gate[e]) * (xs @ w_up[e])
        return hid @ w_down[e]

    yb = lax.map(expert_block, (xb, block_expert))
    y = jnp.zeros_like(xt).at[buf_tok].add(yb.reshape(buf_len, d) * buf_gate[:, None].astype(x.dtype))
    return y.reshape(b, s, d)


def setup_inputs(seed: int = 0) -> dict:
    key = jax.random.key(seed)
    ks = jax.random.split(key, 24)
    f32 = jnp.float32

    def normal(k, shape, scale):
        return jax.random.normal(k, shape, f32) * scale

    dt = jnp.exp(jax.random.uniform(ks[4], (N_EVEN, GDN_HEADS), f32, np.log(1e-3), np.log(1e-1)))
    return {
        'x': normal(ks[0], (BATCH, SEQ, D_MODEL), 1.0),
        'w_in_even': normal(ks[1], (N_EVEN, D_MODEL, IN_EVEN), D_MODEL ** -0.5),
        'conv_w': normal(ks[2], (N_EVEN, GDN_CONV, GDN_QKV), GDN_CONV ** -0.5),
        'a_log': jnp.log(jax.random.uniform(ks[3], (N_EVEN, GDN_HEADS), f32, 1.0, 16.0)),
        'dt_bias': dt + jnp.log(-jnp.expm1(-dt)),
        'gdn_norm_w': 1.0 + normal(ks[5], (N_EVEN, GDN_DV), 0.01),
        'w_out_even': normal(ks[6], (N_EVEN, DSWA_W + GDN_W, D_MODEL), (DSWA_W + GDN_W) ** -0.5 * DEEPNORM_BETA),
        'w_in_odd': normal(ks[7], (N_ODD, D_MODEL, IN_ODD), D_MODEL ** -0.5),
        'b_f': jax.random.uniform(ks[8], (N_ODD, FOX_HEADS), f32, 1.0, 4.0),
        'w_out_odd': normal(ks[9], (N_ODD, FOX_W, D_MODEL), FOX_W ** -0.5 * DEEPNORM_BETA),
        'ln_mix_g': 1.0 + normal(ks[10], (DEPTH, D_MODEL), 0.01),
        'ln_mix_b': normal(ks[11], (DEPTH, D_MODEL), 0.01),
        'ln_ffn_g': 1.0 + normal(ks[12], (DEPTH, D_MODEL), 0.01),
        'ln_ffn_b': normal(ks[13], (DEPTH, D_MODEL), 0.01),
        'w_gate': normal(ks[14], (DEPTH, N_EXPERTS, D_MODEL, D_EXPERT), D_MODEL ** -0.5),
        'w_up': normal(ks[15], (DEPTH, N_EXPERTS, D_MODEL, D_EXPERT), D_MODEL ** -0.5),
        'w_down': normal(ks[16], (DEPTH, N_EXPERTS, D_EXPERT, D_MODEL), D_EXPERT ** -0.5 * DEEPNORM_BETA),
        'router_w': normal(ks[17], (D_MODEL, N_EXPERTS), D_MODEL ** -0.5),
        'router_bias': normal(ks[18], (N_EXPERTS,), 0.01),
    }


def reference(x, w_in_even, conv_w, a_log, dt_bias, gdn_norm_w, w_out_even, w_in_odd, b_f,
              w_out_odd, ln_mix_g, ln_mix_b, ln_ffn_g, ln_ffn_b, w_gate, w_up, w_down,
              router_w, router_bias):
    positions = jnp.arange(x.shape[1], dtype=jnp.int32)
    for layer in range(DEPTH):
        i = layer // 2
        if layer % 2 == 0:
            mix = even_mixer(x, w_in_even[i], conv_w[i], a_log[i], dt_bias[i], gdn_norm_w[i],
                             w_out_even[i], positions)
        else:
            mix = odd_mixer(x, w_in_odd[i], b_f[i], w_out_odd[i])
        x = layer_norm(DEEPNORM_ALPHA * x + mix, ln_mix_g[layer], ln_mix_b[layer])
        ffn = moe_ffn(x, router_w, router_bias, w_gate[layer], w_up[layer], w_down[layer])
        x = layer_norm(DEEPNORM_ALPHA * x + ffn, ln_ffn_g[layer], ln_ffn_b[layer])
    return x
```

```python
import functools

import jax
import jax.numpy as jnp
import numpy as np
from jax import lax
from jax.experimental import pallas as pl
from jax.experimental.pallas import tpu as pltpu

F32 = jnp.float32
BF16 = jnp.bfloat16

V7X_LANES = 128
V7X_VMEM_BYTES = 64 * 1024 * 1024
VMEM_LIMIT = 56 * 1024 * 1024

HEAD_DIM = 128
DSWA_HEADS = 16
DSWA_DILATIONS = (1, 4, 16)
BAND = 128
GDN_HEADS = 16
GDN_CONV = 4
GDN_CHUNK = 128
FOX_HEADS = 32
ROPE_THETA = 500000.0
ROPE_DIM = HEAD_DIM // 4
N_EXPERTS = 32
N_GROUPS = 8
EXPERTS_PER_GROUP = 4
D_EXPERT = 768
MOE_ROWS = 512
LN_EPS = 1e-5
NORM_EPS = 1e-6
NEG_INF = -1e30
DEPTH = 2
DEEPNORM_ALPHA = (2 * DEPTH) ** 0.25


def _params(*sem):
    return pltpu.CompilerParams(dimension_semantics=sem, vmem_limit_bytes=VMEM_LIMIT)


def _mm_kernel(x_ref, w_ref, o_ref, wb_ref):
    @pl.when(pl.program_id(1) == 0)
    def _():
        wb_ref[...] = w_ref[...].astype(BF16)

    o_ref[...] = jnp.dot(x_ref[...], wb_ref[...], preferred_element_type=F32).astype(o_ref.dtype)


def matmul(x, w, *, lead=None, col0=0, ncols=None, out_dtype=F32, tm=1024, tn=512):
    m, k = x.shape
    n_total = w.shape[-1]
    ncols = n_total - col0 if ncols is None else ncols
    tn = min(tn, ncols)
    tm = min(tm, m)
    assert ncols % tn == 0 and col0 % tn == 0 and m % tm == 0
    cb = col0 // tn
    if lead is None:
        w_spec = pl.BlockSpec((k, tn), lambda j, i: (0, j + cb))
    else:
        w_spec = pl.BlockSpec((None, k, tn), lambda j, i: (lead, 0, j + cb))
    return pl.pallas_call(
        _mm_kernel,
        out_shape=jax.ShapeDtypeStruct((m, ncols), out_dtype),
        grid=(ncols // tn, m // tm),
        in_specs=[pl.BlockSpec((tm, k), lambda j, i: (i, 0)), w_spec],
        out_specs=pl.BlockSpec((tm, tn), lambda j, i: (i, j)),
        scratch_shapes=[pltpu.VMEM((k, tn), BF16)],
        compiler_params=_params("arbitrary", "arbitrary"),
        name="matmul",
    )(x, w)


def _ln_kernel(x_ref, mix_ref, g_ref, b_ref, o_ref, ob_ref):
    y = DEEPNORM_ALPHA * x_ref[...] + mix_ref[...]
    mu = jnp.mean(y, axis=-1, keepdims=True)
    yc = y - mu
    var = jnp.mean(yc * yc, axis=-1, keepdims=True)
    out = yc * lax.rsqrt(var + LN_EPS) * g_ref[...] + b_ref[...]
    o_ref[...] = out
    ob_ref[...] = out.astype(BF16)


def residual_layer_norm(x, mix, g, b, layer, *, tm=256):
    m, d = x.shape
    tm = min(tm, m)
    row = pl.BlockSpec((tm, d), lambda i: (i, 0))
    par = pl.BlockSpec((None, 1, d), lambda i: (layer, 0, 0))
    return pl.pallas_call(
        _ln_kernel,
        out_shape=(jax.ShapeDtypeStruct((m, d), F32), jax.ShapeDtypeStruct((m, d), BF16)),
        grid=(m // tm,),
        in_specs=[row, row, par, par],
        out_specs=(row, row),
        compiler_params=_params("arbitrary"),
        name="residual_ln",
    )(x, mix, g.reshape(-1, 1, d), b.reshape(-1, 1, d))


def rotary_tables(seq):
    half = ROPE_DIM // 2
    inv_freq = ROPE_THETA ** (-jnp.arange(half, dtype=F32) / half)
    ang = jnp.arange(seq, dtype=F32)[:, None] * inv_freq[None, :]
    cos, sin = jnp.cos(ang), jnp.sin(ang)
    ones = jnp.ones((seq, HEAD_DIM - ROPE_DIM), F32)
    zeros = jnp.zeros((seq, HEAD_DIM - ROPE_DIM), F32)
    zh = jnp.zeros((seq, half), F32)
    c = jnp.concatenate([cos, cos, ones], axis=-1)
    s_lo = jnp.concatenate([-sin, zh, zeros], axis=-1)
    s_hi = jnp.concatenate([zh, sin, zeros], axis=-1)
    return c, s_lo, s_hi


def _dswa_kernel(q_ref, k_ref, v_ref, c_ref, slo_ref, shi_ref, o_ref, qs, ks, acc, ms, ls, *, seq):
    half = ROPE_DIM // 2
    scale = HEAD_DIM ** -0.5
    rows = 256

    def prep(i, carry):
        sl = pl.ds(pl.multiple_of(i * rows, rows), rows)
        c, slo, shi = c_ref[sl, :], slo_ref[sl, :], shi_ref[sl, :]
        for src, dst in ((q_ref, qs), (k_ref, ks)):
            x = src[sl, :]
            dst[sl, :] = (x * c + pltpu.roll(x, HEAD_DIM - half, 1) * slo
                          + pltpu.roll(x, half, 1) * shi)
        acc[sl, :] = jnp.zeros((rows, HEAD_DIM), F32)
        ls[sl, :] = jnp.zeros((rows, HEAD_DIM), F32)
        ms[sl, :] = jnp.full((rows, HEAD_DIM), NEG_INF, F32)
        return carry

    lax.fori_loop(0, seq // rows, prep, 0)

    qi = lax.broadcasted_iota(jnp.int32, (BAND, 2 * BAND), 0)
    kj = lax.broadcasted_iota(jnp.int32, (BAND, 2 * BAND), 1)
    band_mask = (kj >= qi) & (kj <= qi + BAND)
    first_mask = (lax.broadcasted_iota(jnp.int32, (BAND, BAND), 1)
                  <= lax.broadcasted_iota(jnp.int32, (BAND, BAND), 0))

    def update(rows_q, s, v2):
        m_old = ms[rows_q, :]
        m_new = jnp.maximum(jnp.max(m_old, axis=1, keepdims=True),
                            jnp.max(s, axis=1, keepdims=True))
        alpha = jnp.exp(m_old - m_new)
        p = jnp.exp(s - m_new)
        ls[rows_q, :] = alpha * ls[rows_q, :] + jnp.sum(p, axis=1, keepdims=True)
        acc[rows_q, :] = alpha * acc[rows_q, :] + jnp.dot(
            p.astype(BF16), v2.astype(BF16), preferred_element_type=F32)
        ms[rows_q, :] = jnp.broadcast_to(m_new, (BAND, HEAD_DIM))

    def scores(q, k2):
        return lax.dot_general(q.astype(BF16), k2.astype(BF16), (((1,), (1,)), ((), ())),
                               preferred_element_type=F32) * scale

    for d in DSWA_DILATIONS:
        n_blk = seq // (d * BAND)
        for r in range(d):
            def rows_of(start):
                return pl.ds(start, BAND) if d == 1 else pl.ds(start, BAND, stride=d)

            r0 = rows_of(r)
            s0 = jnp.where(first_mask, scores(qs[r0, :], ks[r0, :]), NEG_INF)
            update(r0, s0, v_ref[r0, :])

            def body(n, carry):
                cur = rows_of(r + n * (d * BAND))
                prev = rows_of(r + (n - 1) * (d * BAND))
                k2 = jnp.concatenate([ks[prev, :], ks[cur, :]], axis=0)
                v2 = jnp.concatenate([v_ref[prev, :], v_ref[cur, :]], axis=0)
                s = jnp.where(band_mask, scores(qs[cur, :], k2), NEG_INF)
                update(cur, s, v2)
                return carry

            lax.fori_loop(1, n_blk, body, 0)

    def fin(i, carry):
        sl = pl.ds(pl.multiple_of(i * rows, rows), rows)
        o_ref[sl, :] = (acc[sl, :] / ls[sl, :]).astype(o_ref.dtype)
        return carry

    lax.fori_loop(0, seq // rows, fin, 0)


def dilated_mixture_attention(h, tables, *, q_col, k_col, v_col, n_heads):
    b, seq, _ = h.shape
    assert seq % (max(DSWA_DILATIONS) * BAND) == 0
    col = lambda c0: pl.BlockSpec((None, seq, HEAD_DIM), lambda bi, hi: (bi, 0, c0 + hi))
    tab = pl.BlockSpec((seq, HEAD_DIM), lambda bi, hi: (0, 0))
    vm = lambda: pltpu.VMEM((seq, HEAD_DIM), F32)
    return pl.pallas_call(
        functools.partial(_dswa_kernel, seq=seq),
        out_shape=jax.ShapeDtypeStruct((b, seq, n_heads * HEAD_DIM), BF16),
        grid=(b, n_heads),
        in_specs=[col(q_col), col(k_col), col(v_col), tab, tab, tab],
        out_specs=pl.BlockSpec((None, seq, HEAD_DIM), lambda bi, hi: (bi, 0, hi)),
        scratch_shapes=[vm(), vm(), vm(), vm(), vm()],
        compiler_params=_params("arbitrary", "arbitrary"),
        name="dswa",
    )(h, h, h, *tables)


def _nt(a, b):
    return lax.dot_general(a.astype(BF16), b.astype(BF16), (((1,), (1,)), ((), ())),
                           preferred_element_type=F32)


def _nn(a, b):
    return jnp.dot(a.astype(BF16), b.astype(BF16), preferred_element_type=F32)


def _gdn_kernel(alog_ref, dtb_ref, q_ref, k_ref, v_ref, gate_ref, cq_ref, ck_ref, cv_ref, ab_ref,
                nw_ref, o_ref, qn, kn, vn, gcs, gct, bts, us, ws, ats, qds, kdt, *, seq):
    hi = pl.program_id(1)
    c = GDN_CHUNK
    rows = 2 * c
    halo = 8
    neg_a = -jnp.exp(jnp.full((1, 1), alog_ref[hi], F32))
    dt_b = dtb_ref[hi]
    row_i = lax.broadcasted_iota(jnp.int32, (c, c), 0)
    col_i = lax.broadcasted_iota(jnp.int32, (c, c), 1)
    tri_incl = (col_i <= row_i).astype(F32)
    eye = (col_i == row_i).astype(F32)
    lane32 = lax.broadcasted_iota(jnp.int32, (rows, 2 * GDN_HEADS), 1)

    def conv_silu(xh, w_ref):
        y = xh[halo:, :] * w_ref[GDN_CONV - 1:GDN_CONV, :]
        for j in range(1, GDN_CONV):
            y = y + pltpu.roll(xh, j, 0)[halo:, :] * w_ref[GDN_CONV - 1 - j:GDN_CONV - j, :]
        return y * (1.0 / (1.0 + jnp.exp(-y)))

    def l2n(t):
        return t * lax.rsqrt(jnp.sum(t * t, axis=-1, keepdims=True) + NORM_EPS)

    def prep(i, first):
        r0 = pl.multiple_of(i * rows, rows)
        sl = pl.ds(r0, rows)

        def with_halo(ref):
            if first:
                return jnp.concatenate([jnp.zeros((halo, HEAD_DIM), F32), ref[sl, :]], axis=0)
            return ref[pl.ds(pl.multiple_of(r0 - halo, halo), rows + halo), :]

        qn[sl, :] = l2n(conv_silu(with_halo(q_ref), cq_ref)) * (HEAD_DIM ** -0.5)
        kn[sl, :] = l2n(conv_silu(with_halo(k_ref), ck_ref))
        vn[sl, :] = conv_silu(with_halo(v_ref), cv_ref)
        ab = ab_ref[sl, :]
        b_logit = jnp.sum(jnp.where(lane32 == hi, ab, 0.0), axis=1, keepdims=True)
        a_logit = jnp.sum(jnp.where(lane32 == hi + GDN_HEADS, ab, 0.0), axis=1, keepdims=True)
        z = a_logit + dt_b
        g = neg_a * (jnp.maximum(z, 0.0) + jnp.log(1.0 + jnp.exp(-jnp.abs(z))))
        bts[sl, :] = jnp.broadcast_to(1.0 / (1.0 + jnp.exp(-b_logit)), (rows, HEAD_DIM))
        gb = jnp.broadcast_to(g, (rows, HEAD_DIM))
        for half in range(rows // c):
            hs = pl.ds(pl.multiple_of(r0 + half * c, c), c)
            gc = jnp.dot(tri_incl, gb[half * c:(half + 1) * c, :], precision=lax.Precision.HIGHEST,
                         preferred_element_type=F32)
            gcs[hs, :] = gc
            gct[hs, :] = gc.T

    prep(0, True)
    lax.fori_loop(1, seq // rows, lambda i, carry: (prep(i, False), carry)[1], 0)

    def independent(n, carry):
        sl = pl.ds(pl.multiple_of(n * c, c), c)
        q, k, v, beta, gc, gcr = qn[sl, :], kn[sl, :], vn[sl, :], bts[sl, :], gcs[sl, :], gct[sl, :]
        decay = jnp.exp(jnp.where(col_i <= row_i, gc - gcr, -jnp.inf))
        kb = k * beta
        nm = _nt(kb, k) * jnp.where(col_i < row_i, decay, 0.0)
        inv = eye - nm
        pw = _nn(nm, nm)
        for lvl in range(6):
            inv = inv + _nn(inv, pw)
            if lvl < 5:
                pw = _nn(pw, pw)
        egc = jnp.exp(gc)
        us[sl, :] = _nn(inv, v * beta)
        ws[sl, :] = _nn(inv, kb * egc).astype(BF16)
        ats[sl, :] = (_nt(q, k) * decay).astype(BF16)
        qds[sl, :] = (q * egc).astype(BF16)
        kdt[sl, :] = (k * jnp.exp(gc[c - 1:c, :] - gc)).T.astype(BF16)
        return carry

    lax.fori_loop(0, seq // c, independent, 0)

    def scan(n, state):
        sl = pl.ds(pl.multiple_of(n * c, c), c)
        v_new = us[sl, :] - _nn(ws[sl, :], state)
        o = _nn(qds[sl, :], state) + _nn(ats[sl, :], v_new)
        last = jnp.exp(gcs[pl.ds(n * c + c - 1, 1), :])
        o = o * lax.rsqrt(jnp.mean(o * o, axis=-1, keepdims=True) + NORM_EPS) * nw_ref[...]
        gate = gate_ref[sl, :]
        o_ref[sl, :] = (o * (gate * (1.0 / (1.0 + jnp.exp(-gate))))).astype(o_ref.dtype)
        return state * last + _nn(kdt[sl, :], v_new)

    lax.fori_loop(0, seq // c, scan, jnp.zeros((HEAD_DIM, HEAD_DIM), F32))


def gated_delta_net(h, ab, conv_w, a_log, dt_bias, norm_w, *, q_col, k_col, v_col, gate_col):
    b, seq, _ = h.shape
    nh = GDN_HEADS
    assert seq % (2 * GDN_CHUNK) == 0
    col = lambda c0: pl.BlockSpec((None, seq, HEAD_DIM), lambda bi, hi, *_: (bi, 0, c0 + hi))
    cw = lambda c0: pl.BlockSpec((GDN_CONV, HEAD_DIM), lambda bi, hi, *_: (0, c0 + hi))
    f32s = lambda: pltpu.VMEM((seq, HEAD_DIM), F32)
    b16s = lambda: pltpu.VMEM((seq, HEAD_DIM), BF16)
    grid_spec = pltpu.PrefetchScalarGridSpec(
        num_scalar_prefetch=2,
        grid=(b, nh),
        in_specs=[col(q_col), col(k_col), col(v_col), col(gate_col), cw(0), cw(nh), cw(2 * nh),
                  pl.BlockSpec((None, seq, 2 * nh), lambda bi, hi, *_: (bi, 0, 0)),
                  pl.BlockSpec((1, HEAD_DIM), lambda bi, hi, *_: (0, 0))],
        out_specs=pl.BlockSpec((None, seq, HEAD_DIM), lambda bi, hi, *_: (bi, 0, hi)),
        scratch_shapes=[f32s(), f32s(), f32s(), f32s(), f32s(), f32s(), f32s(),
                        b16s(), b16s(), b16s(), b16s()],
    )
    return pl.pallas_call(
        functools.partial(_gdn_kernel, seq=seq),
        out_shape=jax.ShapeDtypeStruct((b, seq, nh * HEAD_DIM), BF16),
        grid_spec=grid_spec,
        compiler_params=_params("arbitrary", "arbitrary"),
        name="gdn",
    )(a_log, dt_bias, h, h, h, h, conv_w, conv_w, conv_w, ab, norm_w.reshape(1, HEAD_DIM))


def _fox_cum_kernel(fl_ref, bf_ref, o_ref, *, seq):
    c = 128
    tri_incl = (lax.broadcasted_iota(jnp.int32, (c, c), 1)
                <= lax.broadcasted_iota(jnp.int32, (c, c), 0)).astype(F32)

    def body(n, carry):
        sl = pl.ds(pl.multiple_of(n * c, c), c)
        z = fl_ref[sl, :] + bf_ref[...]
        log_f = -(jnp.maximum(-z, 0.0) + jnp.log(1.0 + jnp.exp(-jnp.abs(z))))
        cs = jnp.dot(tri_incl, log_f, precision=lax.Precision.HIGHEST,
                     preferred_element_type=F32) + carry
        o_ref[sl, :] = cs
        return cs[c - 1:c, :]

    lax.fori_loop(0, seq // c, body, jnp.zeros((1, fl_ref.shape[-1]), F32))


def forget_gate_cumsum(f_logit, b_f):
    b, seq, nh = f_logit.shape
    blk = pl.BlockSpec((None, seq, nh), lambda bi: (bi, 0, 0))
    return pl.pallas_call(
        functools.partial(_fox_cum_kernel, seq=seq),
        out_shape=jax.ShapeDtypeStruct((b, seq, nh), F32),
        grid=(b,),
        in_specs=[blk, pl.BlockSpec((1, nh), lambda bi: (0, 0))],
        out_specs=blk,
        compiler_params=_params("arbitrary"),
        name="fox_cumsum",
    )(f_logit, b_f.reshape(1, nh))


def _fox_kernel(q_ref, k_ref, v_ref, cum_ref, o_ref, *, blk):
    qi = pl.program_id(2)
    scale = HEAD_DIM ** -0.5
    q = q_ref[...]
    ri = lax.broadcasted_iota(jnp.int32, (blk, blk), 0)
    ci = lax.broadcasted_iota(jnp.int32, (blk, blk), 1)
    cq_row = cum_ref[:, pl.ds(pl.multiple_of(qi * blk, blk), blk)]
    cq = jnp.sum(jnp.where(ri == ci, cq_row, 0.0), axis=1, keepdims=True)

    def step(j, carry, diagonal):
        m, l, acc = carry
        ks = pl.ds(pl.multiple_of(j * blk, blk), blk)
        s = _nt(q, k_ref[ks, :]) * scale + (cq - cum_ref[:, ks])
        if diagonal:
            s = jnp.where(ri >= ci, s, NEG_INF)
        m_new = jnp.maximum(m, jnp.max(s, axis=1, keepdims=True))
        alpha = jnp.exp(m - m_new)
        p = jnp.exp(s - m_new)
        l = alpha * l + jnp.sum(p, axis=1, keepdims=True)
        acc = alpha * acc + jnp.dot(p.astype(BF16), v_ref[ks, :], preferred_element_type=F32)
        return m_new, l, acc

    init = (jnp.full((blk, 1), NEG_INF, F32), jnp.zeros((blk, 1), F32),
            jnp.zeros((blk, HEAD_DIM), F32))
    carry = lax.fori_loop(0, qi, lambda j, cr: step(j, cr, False), init)
    _, l, acc = step(qi, carry, True)
    o_ref[...] = (acc / l).astype(o_ref.dtype)


def forgetting_attention(hq, cum_rows, *, n_heads, blk=256):
    b, seq, _ = hq.shape
    assert seq % blk == 0
    kv = lambda c0: pl.BlockSpec((None, seq, HEAD_DIM), lambda bi, hi, qi: (bi, 0, c0 + hi))
    return pl.pallas_call(
        functools.partial(_fox_kernel, blk=blk),
        out_shape=jax.ShapeDtypeStruct((b, seq, n_heads * HEAD_DIM), BF16),
        grid=(b, n_heads, seq // blk),
        in_specs=[pl.BlockSpec((None, blk, HEAD_DIM), lambda bi, hi, qi: (bi, qi, hi)),
                  kv(n_heads), kv(2 * n_heads),
                  pl.BlockSpec((None, None, 1, seq), lambda bi, hi, qi: (bi, hi, 0, 0))],
        out_specs=pl.BlockSpec((None, blk, HEAD_DIM), lambda bi, hi, qi: (bi, qi, hi)),
        compiler_params=_params("arbitrary", "arbitrary", "arbitrary"),
        name="fox",
    )(hq, hq, hq, cum_rows)


def _router_kernel(x_ref, rw_ref, rb_ref, idx_ref, gate_ref):
    logits = lax.dot_general(rw_ref[...], x_ref[...], (((1,), (1,)), ((), ())),
                             precision=lax.Precision.HIGHEST, preferred_element_type=F32)
    score = 1.0 / (1.0 + jnp.exp(-logits))
    sel = score + rb_ref[...]
    sel_r = [sel[e:e + 1, :] for e in range(N_EXPERTS)]
    score_r = [score[e:e + 1, :] for e in range(N_EXPERTS)]

    def top2_sum(a, b, c, d):
        hi1, lo1, hi2, lo2 = jnp.maximum(a, b), jnp.minimum(a, b), jnp.maximum(c, d), jnp.minimum(c, d)
        return jnp.maximum(hi1, hi2) + jnp.maximum(jnp.minimum(hi1, hi2), jnp.maximum(lo1, lo2))

    epg = EXPERTS_PER_GROUP
    best = top2_sum(*sel_r[:epg])
    grp = jnp.zeros_like(best, dtype=jnp.int32)
    for g in range(1, N_GROUPS):
        cand = top2_sum(*sel_r[g * epg:(g + 1) * epg])
        take = cand > best
        best = jnp.where(take, cand, best)
        grp = jnp.where(take, g, grp)
    in_sel, in_score = [], []
    for j in range(epg):
        s_j, u_j = sel_r[j], score_r[j]
        for g in range(1, N_GROUPS):
            s_j = jnp.where(grp == g, sel_r[g * epg + j], s_j)
            u_j = jnp.where(grp == g, score_r[g * epg + j], u_j)
        in_sel.append(s_j)
        in_score.append(u_j)

    def arg_best(vals):
        bv, bi = vals[0], jnp.zeros_like(grp)
        for j in range(1, epg):
            take = vals[j] > bv
            bv = jnp.where(take, vals[j], bv)
            bi = jnp.where(take, j, bi)
        return bi

    i1 = arg_best(in_sel)
    i2 = arg_best([jnp.where(i1 == j, -jnp.inf, in_sel[j]) for j in range(epg)])

    def pick(vals, i):
        out = vals[0]
        for j in range(1, epg):
            out = jnp.where(i == j, vals[j], out)
        return out

    g1, g2 = pick(in_score, i1), pick(in_score, i2)
    tot = g1 + g2
    idx_ref[0:1, :] = grp * epg + i1
    idx_ref[1:2, :] = grp * epg + i2
    gate_ref[0:1, :] = g1 / tot
    gate_ref[1:2, :] = g2 / tot


def route(x, router_w, router_bias, *, tm=512):
    t, d = x.shape
    tm = min(tm, t)
    out = pl.BlockSpec((2, tm), lambda i: (0, i))
    return pl.pallas_call(
        _router_kernel,
        out_shape=(jax.ShapeDtypeStruct((2, t), jnp.int32), jax.ShapeDtypeStruct((2, t), F32)),
        grid=(t // tm,),
        in_specs=[pl.BlockSpec((tm, d), lambda i: (i, 0)),
                  pl.BlockSpec((N_EXPERTS, d), lambda i: (0, 0)),
                  pl.BlockSpec((N_EXPERTS, 1), lambda i: (0, 0))],
        out_specs=(out, out),
        compiler_params=_params("arbitrary"),
        name="router",
    )(x, router_w.T, router_bias.reshape(N_EXPERTS, 1))


def _moe_up_kernel(be_ref, first_ref, nused_ref, x_ref, wg_ref, wu_ref, o_ref, wgb, wub):
    i = pl.program_id(1)

    @pl.when(i < nused_ref[0])
    def _():
        @pl.when(first_ref[i] == 1)
        def _():
            wgb[...] = wg_ref[...].astype(BF16)
            wub[...] = wu_ref[...].astype(BF16)

        xs = x_ref[...]
        a = jnp.dot(xs, wgb[...], preferred_element_type=F32)
        u = jnp.dot(xs, wub[...], preferred_element_type=F32)
        o_ref[...] = (a * (1.0 / (1.0 + jnp.exp(-a))) * u).astype(o_ref.dtype)

    @pl.when(i >= nused_ref[0])
    def _():
        o_ref[...] = jnp.zeros_like(o_ref)


def _moe_down_kernel(be_ref, first_ref, nused_ref, h_ref, wd_ref, o_ref, wdb):
    i = pl.program_id(1)

    @pl.when(i < nused_ref[0])
    def _():
        @pl.when(first_ref[i] == 1)
        def _():
            wdb[...] = wd_ref[...].astype(BF16)

        o_ref[...] = jnp.dot(h_ref[...], wdb[...], preferred_element_type=F32)

    @pl.when(i >= nused_ref[0])
    def _():
        o_ref[...] = jnp.zeros_like(o_ref)


def expert_ffn(xg, block_expert, block_first, n_used, w_gate, w_up, w_down, layer, *, tn_up=256,
               tn_down=1024):
    r, d = xg.shape
    de = w_gate.shape[-1]
    nb = r // MOE_ROWS
    w_in = pl.BlockSpec((None, None, d, tn_up), lambda j, i, be, fi, nu: (layer, be[i], 0, j))
    hid = pl.pallas_call(
        _moe_up_kernel,
        out_shape=jax.ShapeDtypeStruct((r, de), BF16),
        grid_spec=pltpu.PrefetchScalarGridSpec(
            num_scalar_prefetch=3,
            grid=(de // tn_up, nb),
            in_specs=[pl.BlockSpec((MOE_ROWS, d), lambda j, i, *_: (i, 0)), w_in, w_in],
            out_specs=pl.BlockSpec((MOE_ROWS, tn_up), lambda j, i, *_: (i, j)),
            scratch_shapes=[pltpu.VMEM((d, tn_up), BF16), pltpu.VMEM((d, tn_up), BF16)],
        ),
        compiler_params=_params("arbitrary", "arbitrary"),
        name="moe_up",
    )(block_expert, block_first, n_used, xg, w_gate, w_up)
    return pl.pallas_call(
        _moe_down_kernel,
        out_shape=jax.ShapeDtypeStruct((r, d), F32),
        grid_spec=pltpu.PrefetchScalarGridSpec(
            num_scalar_prefetch=3,
            grid=(d // tn_down, nb),
            in_specs=[pl.BlockSpec((MOE_ROWS, de), lambda j, i, *_: (i, 0)),
                      pl.BlockSpec((None, None, de, tn_down),
                                   lambda j, i, be, fi, nu: (layer, be[i], 0, j))],
            out_specs=pl.BlockSpec((MOE_ROWS, tn_down), lambda j, i, *_: (i, j)),
            scratch_shapes=[pltpu.VMEM((de, tn_down), BF16)],
        ),
        compiler_params=_params("arbitrary", "arbitrary"),
        name="moe_down",
    )(block_expert, block_first, n_used, hid, w_down)


def moe_ffn(x, xb, router_w, router_bias, w_gate, w_up, w_down, layer):
    t, d = x.shape
    idx, gate = route(x, router_w, router_bias)
    flat_e = idx.reshape(-1)
    n_assign = flat_e.shape[0]
    counts = jnp.zeros((N_EXPERTS,), jnp.int32).at[flat_e].add(1)
    padded = (counts + MOE_ROWS - 1) // MOE_ROWS * MOE_ROWS
    pad_end = jnp.cumsum(padded)
    pad_start = pad_end - padded
    start = jnp.cumsum(counts) - counts
    order = jnp.argsort(flat_e)
    e_sorted = flat_e[order]
    dest_sorted = pad_start[e_sorted] + jnp.arange(n_assign, dtype=jnp.int32) - start[e_sorted]
    n_blocks = -(-(n_assign + N_EXPERTS * (MOE_ROWS - 1)) // MOE_ROWS)
    rows = n_blocks * MOE_ROWS
    tok_sorted = (order % t).astype(jnp.int32)
    row_tok = jnp.zeros((rows,), jnp.int32).at[dest_sorted].set(tok_sorted)
    pos = jnp.zeros((n_assign,), jnp.int32).at[order].set(dest_sorted.astype(jnp.int32))
    blk_start = jnp.arange(n_blocks, dtype=jnp.int32) * MOE_ROWS
    block_expert = jnp.minimum(jnp.searchsorted(pad_end, blk_start, side='right'),
                               N_EXPERTS - 1).astype(jnp.int32)
    n_used = (pad_end[-1] // MOE_ROWS).astype(jnp.int32).reshape(1)
    block_first = jnp.concatenate([jnp.ones((1,), jnp.int32),
                                   (block_expert[1:] != block_expert[:-1]).astype(jnp.int32)])
    xg = xb[row_tok]
    yg = expert_ffn(xg, block_expert, block_first, n_used, w_gate, w_up, w_down, layer)
    return gate[0][:, None] * yg[pos[:t]] + gate[1][:, None] * yg[pos[t:]]


def kernel(x, w_in_even, conv_w, a_log, dt_bias, gdn_norm_w, w_out_even, w_in_odd, b_f, w_out_odd, ln_mix_g, ln_mix_b, ln_ffn_g, ln_ffn_b, w_gate, w_up, w_down, router_w, router_bias):
    b, seq, d = x.shape
    t = b * seq
    xf = x.reshape(t, d)
    xb = xf.astype(BF16)
    tables = rotary_tables(seq)
    hb = HEAD_DIM
    for layer in range(DEPTH):
        i = layer // 2
        if layer % 2 == 0:
            wa, wqkv, wg = DSWA_HEADS * hb, 3 * GDN_HEADS * hb, GDN_HEADS * hb
            n_main = 3 * wa + wqkv + wg
            h = matmul(xb, w_in_even, lead=i, ncols=n_main).reshape(b, seq, n_main)
            ab = matmul(xb, w_in_even[i, :, n_main:]).reshape(b, seq, 2 * GDN_HEADS)
            o_a = dilated_mixture_attention(h, tables, q_col=0, k_col=DSWA_HEADS,
                                            v_col=2 * DSWA_HEADS, n_heads=DSWA_HEADS)
            c0 = 3 * DSWA_HEADS
            o_b = gated_delta_net(h, ab, conv_w[i], a_log[i], dt_bias[i], gdn_norm_w[i],
                                  q_col=c0, k_col=c0 + GDN_HEADS, v_col=c0 + 2 * GDN_HEADS,
                                  gate_col=c0 + 3 * GDN_HEADS)
            o = jnp.concatenate([o_a, o_b], axis=-1).reshape(t, -1)
            mix = matmul(o, w_out_even, lead=i)
        else:
            n_main = 3 * FOX_HEADS * hb
            hq = matmul(xb, w_in_odd, lead=i, ncols=n_main, out_dtype=BF16).reshape(b, seq, n_main)
            f_logit = matmul(xb, w_in_odd[i, :, n_main:]).reshape(b, seq, FOX_HEADS)
            cum = forget_gate_cumsum(f_logit, b_f[i])
            cum_rows = cum.transpose(0, 2, 1)[:, :, None, :]
            o = forgetting_attention(hq, cum_rows, n_heads=FOX_HEADS).reshape(t, -1)
            mix = matmul(o, w_out_odd, lead=i)
        xf, xb = residual_layer_norm(xf, mix, ln_mix_g, ln_mix_b, layer)
        ffn = moe_ffn(xf, xb, router_w, router_bias, w_gate, w_up, w_down, layer)
        xf, xb = residual_layer_norm(xf, ffn, ln_ffn_g, ln_ffn_b, layer)
    return xf.reshape(b, seq, d)
```

```python
import functools

import jax
import jax.numpy as jnp
import numpy as np
from jax import lax
from jax.experimental import pallas as pl
from jax.experimental.pallas import tpu as pltpu

F32 = jnp.float32
BF16 = jnp.bfloat16

V7X_LANES = 128
V7X_VMEM_BYTES = 64 * 1024 * 1024
VMEM_LIMIT = 56 * 1024 * 1024

HEAD_DIM = 128
DSWA_HEADS = 16
DSWA_DILATIONS = (1, 4, 16)
BAND = 128
GDN_HEADS = 16
GDN_CONV = 4
GDN_CHUNK = 128
GDN_INTERLEAVE = 4
FOX_HEADS = 32
ROPE_THETA = 500000.0
ROPE_DIM = HEAD_DIM // 4
N_EXPERTS = 32
N_GROUPS = 8
EXPERTS_PER_GROUP = 4
D_EXPERT = 768
MOE_ROWS = 512
LN_EPS = 1e-5
NORM_EPS = 1e-6
NEG_INF = -1e30
DEPTH = 2
DEEPNORM_ALPHA = (2 * DEPTH) ** 0.25


def _params(*sem):
    return pltpu.CompilerParams(dimension_semantics=sem, vmem_limit_bytes=VMEM_LIMIT)


def _mm_kernel(x_ref, w_ref, s_ref, o_ref, *scratch):
    if scratch:
        wb_ref, = scratch

        @pl.when(pl.program_id(1) == 0)
        def _():
            wb_ref[...] = w_ref[...].astype(BF16)
    else:
        wb_ref = w_ref
    acc = jnp.dot(x_ref[...], wb_ref[...], preferred_element_type=F32)
    o_ref[...] = (acc * s_ref[...]).astype(o_ref.dtype)


def matmul(x, w, *, lead=None, col0=0, ncols=None, col_scale=None, out_dtype=F32, tm=1024, tn=512):
    m, k = x.shape
    n_total = w.shape[-1]
    ncols = n_total - col0 if ncols is None else ncols
    tn = min(tn, ncols)
    tm = min(tm, m)
    assert ncols % tn == 0 and col0 % tn == 0 and m % tm == 0
    cb = col0 // tn
    if lead is None:
        w_spec = pl.BlockSpec((k, tn), lambda j, i: (0, j + cb))
    else:
        w_spec = pl.BlockSpec((None, k, tn), lambda j, i: (lead, 0, j + cb))
    if col_scale is None:
        col_scale = jnp.ones((1, ncols), F32)
    return pl.pallas_call(
        _mm_kernel,
        out_shape=jax.ShapeDtypeStruct((m, ncols), out_dtype),
        grid=(ncols // tn, m // tm),
        in_specs=[pl.BlockSpec((tm, k), lambda j, i: (i, 0)), w_spec,
                  pl.BlockSpec((1, tn), lambda j, i: (0, j))],
        out_specs=pl.BlockSpec((tm, tn), lambda j, i: (i, j)),
        scratch_shapes=[] if w.dtype == BF16 else [pltpu.VMEM((k, tn), BF16)],
        compiler_params=_params("arbitrary", "arbitrary"),
        name="matmul",
    )(x, w, col_scale)


def _ln_kernel(x_ref, mix_ref, g_ref, b_ref, o_ref, ob_ref):
    y = DEEPNORM_ALPHA * x_ref[...] + mix_ref[...]
    mu = jnp.mean(y, axis=-1, keepdims=True)
    yc = y - mu
    var = jnp.mean(yc * yc, axis=-1, keepdims=True)
    out = yc * lax.rsqrt(var + LN_EPS) * g_ref[...] + b_ref[...]
    o_ref[...] = out
    ob_ref[...] = out.astype(BF16)


def residual_layer_norm(x, mix, g, b, layer, *, tm=256):
    m, d = x.shape
    tm = min(tm, m)
    row = pl.BlockSpec((tm, d), lambda i: (i, 0))
    par = pl.BlockSpec((None, 1, d), lambda i: (layer, 0, 0))
    return pl.pallas_call(
        _ln_kernel,
        out_shape=(jax.ShapeDtypeStruct((m, d), F32), jax.ShapeDtypeStruct((m, d), BF16)),
        grid=(m // tm,),
        in_specs=[row, row, par, par],
        out_specs=(row, row),
        compiler_params=_params("arbitrary"),
        name="residual_ln",
    )(x, mix, g.reshape(-1, 1, d), b.reshape(-1, 1, d))


def rotary_tables(seq):
    half = ROPE_DIM // 2
    inv_freq = ROPE_THETA ** (-jnp.arange(half, dtype=F32) / half)
    ang = jnp.arange(seq, dtype=F32)[:, None] * inv_freq[None, :]
    cos, sin = jnp.cos(ang), jnp.sin(ang)
    ones = jnp.ones((seq, HEAD_DIM - ROPE_DIM), F32)
    zeros = jnp.zeros((seq, HEAD_DIM - ROPE_DIM), F32)
    zh = jnp.zeros((seq, half), F32)
    c = jnp.concatenate([cos, cos, ones], axis=-1)
    s_lo = jnp.concatenate([-sin, zh, zeros], axis=-1)
    s_hi = jnp.concatenate([zh, sin, zeros], axis=-1)
    return c, s_lo, s_hi


def _dswa_kernel(q_ref, k_ref, v_ref, c_ref, slo_ref, shi_ref, o_ref, qs, ks, acc, ms, ls, *, seq):
    half = ROPE_DIM // 2
    rows = 256

    def prep(i, carry):
        sl = pl.ds(pl.multiple_of(i * rows, rows), rows)
        c, slo, shi = c_ref[sl, :], slo_ref[sl, :], shi_ref[sl, :]
        for src, dst in ((q_ref, qs), (k_ref, ks)):
            x = src[sl, :]
            dst[sl, :] = (x * c + pltpu.roll(x, HEAD_DIM - half, 1) * slo
                          + pltpu.roll(x, half, 1) * shi)
        acc[sl, :] = jnp.zeros((rows, HEAD_DIM), F32)
        ls[sl, :] = jnp.zeros((rows, HEAD_DIM), F32)
        ms[sl, :] = jnp.full((rows, HEAD_DIM), NEG_INF, F32)
        return carry

    lax.fori_loop(0, seq // rows, prep, 0)

    qi = lax.broadcasted_iota(jnp.int32, (BAND, 2 * BAND), 0)
    kj = lax.broadcasted_iota(jnp.int32, (BAND, 2 * BAND), 1)
    band_mask = (kj >= qi) & (kj <= qi + BAND)
    first_mask = (lax.broadcasted_iota(jnp.int32, (BAND, BAND), 1)
                  <= lax.broadcasted_iota(jnp.int32, (BAND, BAND), 0))

    def scores(q, k2):
        return lax.dot_general(q.astype(BF16), k2.astype(BF16), (((1,), (1,)), ((), ())),
                               preferred_element_type=F32)

    def process(cur, prev):
        if prev is None:
            k2, v2, mask = ks[cur, :], v_ref[cur, :], first_mask
        else:
            k2 = jnp.concatenate([ks[prev, :], ks[cur, :]], axis=0)
            v2 = jnp.concatenate([v_ref[prev, :], v_ref[cur, :]], axis=0)
            mask = band_mask
        s = jnp.where(mask, scores(qs[cur, :], k2), NEG_INF)
        m_old = ms[cur, :]
        m_new = jnp.maximum(m_old[:, :1], jnp.max(s, axis=1, keepdims=True))
        alpha = jnp.exp(m_old - m_new)
        p = jnp.exp(s - m_new)
        ls[cur, :] = alpha * ls[cur, :] + jnp.sum(p, axis=1, keepdims=True)
        acc[cur, :] = alpha * acc[cur, :] + jnp.dot(p.astype(BF16), v2.astype(BF16),
                                                    preferred_element_type=F32)
        ms[cur, :] = jnp.broadcast_to(m_new, (BAND, HEAD_DIM))

    for d in DSWA_DILATIONS:
        n_blk = seq // (d * BAND)
        step = d * BAND
        for r in range(d):
            def rows_of(start):
                return pl.ds(start, BAND) if d == 1 else pl.ds(start, BAND, stride=d)

            def body(n, carry):
                process(rows_of(r + n * step), rows_of(r + (n - 1) * step))
                return carry

            process(rows_of(r), None)
            lax.fori_loop(1, n_blk, body, 0, unroll=2)

    def fin(i, carry):
        sl = pl.ds(pl.multiple_of(i * rows, rows), rows)
        o_ref[sl, :] = (acc[sl, :] / ls[sl, :]).astype(o_ref.dtype)
        return carry

    lax.fori_loop(0, seq // rows, fin, 0)


def dilated_mixture_attention(h, tables, *, q_col, k_col, v_col, n_heads):
    b, seq, _ = h.shape
    assert seq % (max(DSWA_DILATIONS) * BAND) == 0
    col = lambda c0: pl.BlockSpec((None, seq, HEAD_DIM), lambda bi, hi: (bi, 0, c0 + hi))
    tab = pl.BlockSpec((seq, HEAD_DIM), lambda bi, hi: (0, 0))
    vm = lambda: pltpu.VMEM((seq, HEAD_DIM), F32)
    return pl.pallas_call(
        functools.partial(_dswa_kernel, seq=seq),
        out_shape=jax.ShapeDtypeStruct((b, seq, n_heads * HEAD_DIM), BF16),
        grid=(b, n_heads),
        in_specs=[col(q_col), col(k_col), col(v_col), tab, tab, tab],
        out_specs=pl.BlockSpec((None, seq, HEAD_DIM), lambda bi, hi: (bi, 0, hi)),
        scratch_shapes=[vm(), vm(), vm(), vm(), vm()],
        compiler_params=_params("arbitrary", "arbitrary"),
        name="dswa",
    )(h, h, h, *tables)


def _nt(a, b):
    return lax.dot_general(a.astype(BF16), b.astype(BF16), (((1,), (1,)), ((), ())),
                           preferred_element_type=F32)


def _nn(a, b):
    return jnp.dot(a.astype(BF16), b.astype(BF16), preferred_element_type=F32)


def _gdn_kernel(alog_ref, dtb_ref, q_ref, k_ref, v_ref, gate_ref, cq_ref, ck_ref, cv_ref, ab_ref,
                nw_ref, o_ref, qn, kn, vn, gcs, gct, bts, us, wq, ak, *, seq):
    hi = pl.program_id(1)
    c = GDN_CHUNK
    rows = 2 * c
    halo = 8
    neg_a = -jnp.exp(jnp.full((1, 1), alog_ref[hi], F32))
    dt_b = dtb_ref[hi]
    row_i = lax.broadcasted_iota(jnp.int32, (c, c), 0)
    col_i = lax.broadcasted_iota(jnp.int32, (c, c), 1)
    tri_incl = (col_i <= row_i).astype(F32)
    eye = (col_i == row_i).astype(F32)
    lane32 = lax.broadcasted_iota(jnp.int32, (rows, 2 * GDN_HEADS), 1)

    def conv_silu(xh, w_ref):
        y = xh[halo:, :] * w_ref[GDN_CONV - 1:GDN_CONV, :]
        for j in range(1, GDN_CONV):
            y = y + pltpu.roll(xh, j, 0)[halo:, :] * w_ref[GDN_CONV - 1 - j:GDN_CONV - j, :]
        return y * (1.0 / (1.0 + jnp.exp(-y)))

    def l2n(t):
        return t * lax.rsqrt(jnp.sum(t * t, axis=-1, keepdims=True) + NORM_EPS)

    def prep(i, first):
        r0 = pl.multiple_of(i * rows, rows)
        sl = pl.ds(r0, rows)

        def with_halo(ref):
            if first:
                return jnp.concatenate([jnp.zeros((halo, HEAD_DIM), F32), ref[sl, :]], axis=0)
            return ref[pl.ds(pl.multiple_of(r0 - halo, halo), rows + halo), :]

        qn[sl, :] = l2n(conv_silu(with_halo(q_ref), cq_ref)) * (HEAD_DIM ** -0.5)
        kn[sl, :] = l2n(conv_silu(with_halo(k_ref), ck_ref))
        vn[sl, :] = conv_silu(with_halo(v_ref), cv_ref)
        ab = ab_ref[sl, :]
        b_logit = jnp.sum(jnp.where(lane32 == hi, ab, 0.0), axis=1, keepdims=True)
        a_logit = jnp.sum(jnp.where(lane32 == hi + GDN_HEADS, ab, 0.0), axis=1, keepdims=True)
        z = a_logit + dt_b
        g = neg_a * (jnp.maximum(z, 0.0) + jnp.log(1.0 + jnp.exp(-jnp.abs(z))))
        bts[sl, :] = jnp.broadcast_to(1.0 / (1.0 + jnp.exp(-b_logit)), (rows, HEAD_DIM))
        gb = jnp.broadcast_to(g, (rows, HEAD_DIM))
        for half in range(rows // c):
            hs = pl.ds(pl.multiple_of(r0 + half * c, c), c)
            gc = jnp.dot(tri_incl, gb[half * c:(half + 1) * c, :], precision=lax.Precision.HIGHEST,
                         preferred_element_type=F32)
            gcs[hs, :] = gc
            gct[hs, :] = gc.T

    prep(0, True)
    lax.fori_loop(1, seq // rows, lambda i, carry: (prep(i, False), carry)[1], 0)

    def independent(q, k, v, beta, gc, gcr):
        decay = jnp.exp(jnp.where(col_i <= row_i, gc - gcr, -jnp.inf))
        kb = k * beta
        kq = _nt(jnp.concatenate([kb, q], axis=0), k)
        nm = kq[:c, :] * jnp.where(col_i < row_i, decay, 0.0)
        attn = kq[c:, :] * decay
        inv = eye - nm
        pw = _nn(nm, nm)
        for lvl in range(6):
            if lvl < 5:
                both = _nn(jnp.concatenate([inv, pw], axis=0), pw)
                inv, pw = inv + both[:c, :], both[c:, :]
            else:
                inv = inv + _nn(inv, pw)
        egc = jnp.exp(gc)
        uw = _nn(inv, jnp.concatenate([v * beta, kb * egc], axis=1))
        return (uw[:, :HEAD_DIM],
                jnp.concatenate([uw[:, HEAD_DIM:], q * egc], axis=0).astype(BF16),
                jnp.concatenate([attn, (k * jnp.exp(gc[c - 1:c, :] - gc)).T], axis=0).astype(BF16))

    group = GDN_INTERLEAVE

    def independent_group(i, carry):
        sls = [pl.ds(pl.multiple_of((i * group + u) * c, c), c) for u in range(group)]
        s2s = [pl.ds(pl.multiple_of((i * group + u) * (2 * c), 2 * c), 2 * c) for u in range(group)]
        loaded = [tuple(ref[sl, :] for ref in (qn, kn, vn, bts, gcs, gct)) for sl in sls]
        results = [independent(*args) for args in loaded]
        for sl, s2, (u_new, wq_new, ak_new) in zip(sls, s2s, results):
            us[sl, :] = u_new
            wq[s2, :] = wq_new
            ak[s2, :] = ak_new
        return carry

    lax.fori_loop(0, seq // (c * group), independent_group, 0)

    def scan(n, state):
        sl = pl.ds(pl.multiple_of(n * c, c), c)
        s2 = pl.ds(pl.multiple_of(n * (2 * c), 2 * c), 2 * c)
        ws_qs = _nn(wq[s2, :], state)
        v_new = us[sl, :] - ws_qs[:c, :]
        av_kv = _nn(ak[s2, :], v_new)
        o = ws_qs[c:, :] + av_kv[:c, :]
        last = jnp.exp(gcs[pl.ds(n * c + c - 1, 1), :])
        o = o * lax.rsqrt(jnp.mean(o * o, axis=-1, keepdims=True) + NORM_EPS) * nw_ref[...]
        gate = gate_ref[sl, :]
        o_ref[sl, :] = (o * (gate * (1.0 / (1.0 + jnp.exp(-gate))))).astype(o_ref.dtype)
        return state * last + av_kv[c:, :]

    lax.fori_loop(0, seq // c, scan, jnp.zeros((HEAD_DIM, HEAD_DIM), F32))


def gated_delta_net(h, ab, conv_w, a_log, dt_bias, norm_w, *, q_col, k_col, v_col, gate_col):
    b, seq, _ = h.shape
    nh = GDN_HEADS
    assert seq % (max(2, GDN_INTERLEAVE) * GDN_CHUNK) == 0
    col = lambda c0: pl.BlockSpec((None, seq, HEAD_DIM), lambda bi, hi, *_: (bi, 0, c0 + hi))
    cw = lambda c0: pl.BlockSpec((GDN_CONV, HEAD_DIM), lambda bi, hi, *_: (0, c0 + hi))
    f32s = lambda: pltpu.VMEM((seq, HEAD_DIM), F32)
    b16s = lambda: pltpu.VMEM((2 * seq, HEAD_DIM), BF16)
    grid_spec = pltpu.PrefetchScalarGridSpec(
        num_scalar_prefetch=2,
        grid=(b, nh),
        in_specs=[col(q_col), col(k_col), col(v_col), col(gate_col), cw(0), cw(nh), cw(2 * nh),
                  pl.BlockSpec((None, seq, 2 * nh), lambda bi, hi, *_: (bi, 0, 0)),
                  pl.BlockSpec((1, HEAD_DIM), lambda bi, hi, *_: (0, 0))],
        out_specs=pl.BlockSpec((None, seq, HEAD_DIM), lambda bi, hi, *_: (bi, 0, hi)),
        scratch_shapes=[f32s(), f32s(), f32s(), f32s(), f32s(), f32s(), f32s(), b16s(), b16s()],
    )
    return pl.pallas_call(
        functools.partial(_gdn_kernel, seq=seq),
        out_shape=jax.ShapeDtypeStruct((b, seq, nh * HEAD_DIM), BF16),
        grid_spec=grid_spec,
        compiler_params=_params("arbitrary", "arbitrary"),
        name="gdn",
    )(a_log, dt_bias, h, h, h, h, conv_w, conv_w, conv_w, ab, norm_w.reshape(1, HEAD_DIM))


def _fox_cum_kernel(fl_ref, bf_ref, o_ref, *, seq):
    c = 128
    tri_incl = (lax.broadcasted_iota(jnp.int32, (c, c), 1)
                <= lax.broadcasted_iota(jnp.int32, (c, c), 0)).astype(F32)

    def body(n, carry):
        sl = pl.ds(pl.multiple_of(n * c, c), c)
        z = fl_ref[sl, :] + bf_ref[...]
        log_f = -(jnp.maximum(-z, 0.0) + jnp.log(1.0 + jnp.exp(-jnp.abs(z))))
        cs = jnp.dot(tri_incl, log_f, precision=lax.Precision.HIGHEST,
                     preferred_element_type=F32) + carry
        o_ref[sl, :] = cs
        return cs[c - 1:c, :]

    lax.fori_loop(0, seq // c, body, jnp.zeros((1, fl_ref.shape[-1]), F32))


def forget_gate_cumsum(f_logit, b_f):
    b, seq, nh = f_logit.shape
    blk = pl.BlockSpec((None, seq, nh), lambda bi: (bi, 0, 0))
    return pl.pallas_call(
        functools.partial(_fox_cum_kernel, seq=seq),
        out_shape=jax.ShapeDtypeStruct((b, seq, nh), F32),
        grid=(b,),
        in_specs=[blk, pl.BlockSpec((1, nh), lambda bi: (0, 0))],
        out_specs=blk,
        compiler_params=_params("arbitrary"),
        name="fox_cumsum",
    )(f_logit, b_f.reshape(1, nh))


def _fox_kernel(q_ref, k_ref, v_ref, cum_ref, o_ref, *, blk, heads):
    qi = pl.program_id(2)
    hd = HEAD_DIM
    ri = lax.broadcasted_iota(jnp.int32, (blk, blk), 0)
    ci = lax.broadcasted_iota(jnp.int32, (blk, blk), 1)
    eye = (lax.broadcasted_iota(jnp.int32, (hd, hd), 0)
           == lax.broadcasted_iota(jnp.int32, (hd, hd), 1))
    qs = pl.ds(pl.multiple_of(qi * blk, blk), blk)

    def as_column(row):
        return jnp.concatenate(
            [jnp.sum(jnp.where(eye, row[:, c * hd:(c + 1) * hd], 0.0), axis=1, keepdims=True)
             for c in range(blk // hd)], axis=0)

    cq = [as_column(cum_ref[g:g + 1, qs]) for g in range(heads)]
    q = [q_ref[:, g * hd:(g + 1) * hd] for g in range(heads)]

    def step(j, carry, diagonal):
        ks = pl.ds(pl.multiple_of(j * blk, blk), blk)
        out = []
        for g in range(heads):
            m, l, acc = carry[g]
            s = _nt(q[g], k_ref[ks, g * hd:(g + 1) * hd]) + (cq[g] - cum_ref[g:g + 1, ks])
            if diagonal:
                s = jnp.where(ri >= ci, s, NEG_INF)
            m_new = jnp.maximum(m, jnp.max(s, axis=1, keepdims=True))
            alpha = jnp.exp(m - m_new)
            p = jnp.exp(s - m_new)
            l = alpha * l + jnp.sum(p, axis=1, keepdims=True)
            acc = alpha * acc + jnp.dot(p.astype(BF16), v_ref[ks, g * hd:(g + 1) * hd],
                                        preferred_element_type=F32)
            out.append((m_new, l, acc))
        return tuple(out)

    init = tuple((jnp.full((blk, 1), NEG_INF, F32), jnp.zeros((blk, 1), F32),
                  jnp.zeros((blk, hd), F32)) for _ in range(heads))
    carry = lax.fori_loop(0, qi, lambda j, cr: step(j, cr, False), init)
    carry = step(qi, carry, True)
    for g in range(heads):
        _, l, acc = carry[g]
        o_ref[:, g * hd:(g + 1) * hd] = (acc / l).astype(o_ref.dtype)


def forgetting_attention(hq, cum_rows, *, n_heads, blk=512, heads=2):
    b, seq, _ = hq.shape
    assert seq % blk == 0 and n_heads % heads == 0
    w = heads * HEAD_DIM
    ng = n_heads // heads
    kv = lambda c0: pl.BlockSpec((None, seq, w), lambda bi, hi, qi: (bi, 0, c0 + hi))
    return pl.pallas_call(
        functools.partial(_fox_kernel, blk=blk, heads=heads),
        out_shape=jax.ShapeDtypeStruct((b, seq, n_heads * HEAD_DIM), BF16),
        grid=(b, ng, seq // blk),
        in_specs=[pl.BlockSpec((None, blk, w), lambda bi, hi, qi: (bi, qi, hi)),
                  kv(ng), kv(2 * ng),
                  pl.BlockSpec((None, None, heads, seq), lambda bi, hi, qi: (bi, hi, 0, 0))],
        out_specs=pl.BlockSpec((None, blk, w), lambda bi, hi, qi: (bi, qi, hi)),
        compiler_params=_params("arbitrary", "arbitrary", "arbitrary"),
        name="fox",
    )(hq, hq, hq, cum_rows)


def _router_kernel(x_ref, rw_ref, rb_ref, idx_ref, gate_ref):
    logits = lax.dot_general(rw_ref[...], x_ref[...], (((1,), (1,)), ((), ())),
                             precision=lax.Precision.HIGHEST, preferred_element_type=F32)
    score = 1.0 / (1.0 + jnp.exp(-logits))
    sel = score + rb_ref[...]
    sel_r = [sel[e:e + 1, :] for e in range(N_EXPERTS)]
    score_r = [score[e:e + 1, :] for e in range(N_EXPERTS)]

    def top2_sum(a, b, c, d):
        hi1, lo1, hi2, lo2 = jnp.maximum(a, b), jnp.minimum(a, b), jnp.maximum(c, d), jnp.minimum(c, d)
        return jnp.maximum(hi1, hi2) + jnp.maximum(jnp.minimum(hi1, hi2), jnp.maximum(lo1, lo2))

    epg = EXPERTS_PER_GROUP
    best = top2_sum(*sel_r[:epg])
    grp = jnp.zeros_like(best, dtype=jnp.int32)
    for g in range(1, N_GROUPS):
        cand = top2_sum(*sel_r[g * epg:(g + 1) * epg])
        take = cand > best
        best = jnp.where(take, cand, best)
        grp = jnp.where(take, g, grp)
    in_sel, in_score = [], []
    for j in range(epg):
        s_j, u_j = sel_r[j], score_r[j]
        for g in range(1, N_GROUPS):
            s_j = jnp.where(grp == g, sel_r[g * epg + j], s_j)
            u_j = jnp.where(grp == g, score_r[g * epg + j], u_j)
        in_sel.append(s_j)
        in_score.append(u_j)

    def arg_best(vals):
        bv, bi = vals[0], jnp.zeros_like(grp)
        for j in range(1, epg):
            take = vals[j] > bv
            bv = jnp.where(take, vals[j], bv)
            bi = jnp.where(take, j, bi)
        return bi

    i1 = arg_best(in_sel)
    i2 = arg_best([jnp.where(i1 == j, -jnp.inf, in_sel[j]) for j in range(epg)])

    def pick(vals, i):
        out = vals[0]
        for j in range(1, epg):
            out = jnp.where(i == j, vals[j], out)
        return out

    g1, g2 = pick(in_score, i1), pick(in_score, i2)
    tot = g1 + g2
    idx_ref[0:1, :] = grp * epg + i1
    idx_ref[1:2, :] = grp * epg + i2
    gate_ref[0:1, :] = g1 / tot
    gate_ref[1:2, :] = g2 / tot


def route(x, router_w, router_bias, *, tm=512):
    t, d = x.shape
    tm = min(tm, t)
    out = pl.BlockSpec((2, tm), lambda i: (0, i))
    return pl.pallas_call(
        _router_kernel,
        out_shape=(jax.ShapeDtypeStruct((2, t), jnp.int32), jax.ShapeDtypeStruct((2, t), F32)),
        grid=(t // tm,),
        in_specs=[pl.BlockSpec((tm, d), lambda i: (i, 0)),
                  pl.BlockSpec((N_EXPERTS, d), lambda i: (0, 0)),
                  pl.BlockSpec((N_EXPERTS, 1), lambda i: (0, 0))],
        out_specs=(out, out),
        compiler_params=_params("arbitrary"),
        name="router",
    )(x, router_w.T, router_bias.reshape(N_EXPERTS, 1))


def _moe_up_kernel(be_ref, first_ref, nused_ref, x_ref, wg_ref, wu_ref, o_ref, wgb, wub):
    i = pl.program_id(1)

    @pl.when(i < nused_ref[0])
    def _():
        @pl.when(first_ref[i] == 1)
        def _():
            wgb[...] = wg_ref[...].astype(BF16)
            wub[...] = wu_ref[...].astype(BF16)

        xs = x_ref[...]
        a = jnp.dot(xs, wgb[...], preferred_element_type=F32)
        u = jnp.dot(xs, wub[...], preferred_element_type=F32)
        o_ref[...] = (a * (1.0 / (1.0 + jnp.exp(-a))) * u).astype(o_ref.dtype)

    @pl.when(i >= nused_ref[0])
    def _():
        o_ref[...] = jnp.zeros_like(o_ref)


def _moe_down_kernel(be_ref, first_ref, nused_ref, h_ref, wd_ref, o_ref, wdb):
    i = pl.program_id(1)

    @pl.when(i < nused_ref[0])
    def _():
        @pl.when(first_ref[i] == 1)
        def _():
            wdb[...] = wd_ref[...].astype(BF16)

        o_ref[...] = jnp.dot(h_ref[...], wdb[...], preferred_element_type=F32)

    @pl.when(i >= nused_ref[0])
    def _():
        o_ref[...] = jnp.zeros_like(o_ref)


def expert_ffn(xg, block_expert, block_first, n_used, w_gate, w_up, w_down, layer, *, tn_up=256,
               tn_down=1024):
    r, d = xg.shape
    de = w_gate.shape[-1]
    nb = r // MOE_ROWS
    w_in = pl.BlockSpec((None, None, d, tn_up), lambda j, i, be, fi, nu: (layer, be[i], 0, j))
    hid = pl.pallas_call(
        _moe_up_kernel,
        out_shape=jax.ShapeDtypeStruct((r, de), BF16),
        grid_spec=pltpu.PrefetchScalarGridSpec(
            num_scalar_prefetch=3,
            grid=(de // tn_up, nb),
            in_specs=[pl.BlockSpec((MOE_ROWS, d), lambda j, i, *_: (i, 0)), w_in, w_in],
            out_specs=pl.BlockSpec((MOE_ROWS, tn_up), lambda j, i, *_: (i, j)),
            scratch_shapes=[pltpu.VMEM((d, tn_up), BF16), pltpu.VMEM((d, tn_up), BF16)],
        ),
        compiler_params=_params("arbitrary", "arbitrary"),
        name="moe_up",
    )(block_expert, block_first, n_used, xg, w_gate, w_up)
    return pl.pallas_call(
        _moe_down_kernel,
        out_shape=jax.ShapeDtypeStruct((r, d), F32),
        grid_spec=pltpu.PrefetchScalarGridSpec(
            num_scalar_prefetch=3,
            grid=(d // tn_down, nb),
            in_specs=[pl.BlockSpec((MOE_ROWS, de), lambda j, i, *_: (i, 0)),
                      pl.BlockSpec((None, None, de, tn_down),
                                   lambda j, i, be, fi, nu: (layer, be[i], 0, j))],
            out_specs=pl.BlockSpec((MOE_ROWS, tn_down), lambda j, i, *_: (i, j)),
            scratch_shapes=[pltpu.VMEM((de, tn_down), BF16)],
        ),
        compiler_params=_params("arbitrary", "arbitrary"),
        name="moe_down",
    )(block_expert, block_first, n_used, hid, w_down)


def _gather_kernel(idx_ref, src_ref, o_ref, sem, *, rows):
    base = pl.program_id(0) * rows

    def row_copy(r, src_row):
        return pltpu.make_async_copy(src_ref.at[pl.ds(src_row, 1)], o_ref.at[pl.ds(r, 1)], sem)

    def issue(r, carry):
        row_copy(r, idx_ref[base + r]).start()
        return carry

    def drain(r, carry):
        row_copy(r, 0).wait()
        return carry

    lax.fori_loop(0, rows, issue, 0, unroll=8)
    lax.fori_loop(0, rows, drain, 0, unroll=8)


def gather_rows(src, idx, *, rows=MOE_ROWS):
    m = idx.shape[0]
    assert m % rows == 0
    _, s, lanes = src.shape
    return pl.pallas_call(
        functools.partial(_gather_kernel, rows=rows),
        out_shape=jax.ShapeDtypeStruct((m, s, lanes), src.dtype),
        grid_spec=pltpu.PrefetchScalarGridSpec(
            num_scalar_prefetch=1,
            grid=(m // rows,),
            in_specs=[pl.BlockSpec(memory_space=pl.ANY)],
            out_specs=pl.BlockSpec((rows, s, lanes), lambda i, idx_ref: (i, 0, 0)),
            scratch_shapes=[pltpu.SemaphoreType.DMA(())],
        ),
        compiler_params=_params("arbitrary"),
        name="gather_rows",
    )(idx, src)


def moe_ffn(x, xb, router_w, router_bias, w_gate, w_up, w_down, layer):
    t, d = x.shape
    idx, gate = route(x, router_w, router_bias)
    flat_e = idx.reshape(-1)
    n_assign = flat_e.shape[0]
    counts = jnp.zeros((N_EXPERTS,), jnp.int32).at[flat_e].add(1)
    padded = (counts + MOE_ROWS - 1) // MOE_ROWS * MOE_ROWS
    pad_end = jnp.cumsum(padded)
    pad_start = pad_end - padded
    start = jnp.cumsum(counts) - counts
    order = jnp.argsort(flat_e)
    e_sorted = flat_e[order]
    dest_sorted = pad_start[e_sorted] + jnp.arange(n_assign, dtype=jnp.int32) - start[e_sorted]
    n_blocks = -(-(n_assign + N_EXPERTS * (MOE_ROWS - 1)) // MOE_ROWS)
    rows = n_blocks * MOE_ROWS
    tok_sorted = (order % t).astype(jnp.int32)
    row_tok = jnp.zeros((rows,), jnp.int32).at[dest_sorted].set(tok_sorted)
    pos = jnp.zeros((n_assign,), jnp.int32).at[order].set(dest_sorted.astype(jnp.int32))
    blk_start = jnp.arange(n_blocks, dtype=jnp.int32) * MOE_ROWS
    block_expert = jnp.minimum(jnp.searchsorted(pad_end, blk_start, side='right'),
                               N_EXPERTS - 1).astype(jnp.int32)
    n_used = (pad_end[-1] // MOE_ROWS).astype(jnp.int32).reshape(1)
    block_first = jnp.concatenate([jnp.ones((1,), jnp.int32),
                                   (block_expert[1:] != block_expert[:-1]).astype(jnp.int32)])
    xg = gather_rows(xb.reshape(t, d // V7X_LANES, V7X_LANES), row_tok).reshape(rows, d)
    yg = expert_ffn(xg, block_expert, block_first, n_used, w_gate, w_up, w_down, layer)
    return gate[0][:, None] * yg[pos[:t]] + gate[1][:, None] * yg[pos[t:]]


def _q_scale(n_q, n_total):
    return jnp.concatenate([jnp.full((1, n_q), HEAD_DIM ** -0.5, F32),
                            jnp.ones((1, n_total - n_q), F32)], axis=1)


def kernel(x, w_in_even, conv_w, a_log, dt_bias, gdn_norm_w, w_out_even, w_in_odd, b_f, w_out_odd, ln_mix_g, ln_mix_b, ln_ffn_g, ln_ffn_b, w_gate, w_up, w_down, router_w, router_bias):
    b, seq, d = x.shape
    t = b * seq
    xf = x.reshape(t, d)
    xb = xf.astype(BF16)
    tables = rotary_tables(seq)
    hb = HEAD_DIM
    for layer in range(DEPTH):
        i = layer // 2
        if layer % 2 == 0:
            wa, wqkv, wg = DSWA_HEADS * hb, 3 * GDN_HEADS * hb, GDN_HEADS * hb
            n_main = 3 * wa + wqkv + wg
            w_in = w_in_even[i].astype(BF16)
            h = matmul(xb, w_in, ncols=n_main, col_scale=_q_scale(wa, n_main)).reshape(b, seq, n_main)
            ab = matmul(xb, w_in[:, n_main:]).reshape(b, seq, 2 * GDN_HEADS)
            o_a = dilated_mixture_attention(h, tables, q_col=0, k_col=DSWA_HEADS,
                                            v_col=2 * DSWA_HEADS, n_heads=DSWA_HEADS)
            c0 = 3 * DSWA_HEADS
            o_b = gated_delta_net(h, ab, conv_w[i], a_log[i], dt_bias[i], gdn_norm_w[i],
                                  q_col=c0, k_col=c0 + GDN_HEADS, v_col=c0 + 2 * GDN_HEADS,
                                  gate_col=c0 + 3 * GDN_HEADS)
            o = jnp.concatenate([o_a, o_b], axis=-1).reshape(t, -1)
            mix = matmul(o, w_out_even, lead=i)
        else:
            n_main = 3 * FOX_HEADS * hb
            w_in = w_in_odd[i].astype(BF16)
            hq = matmul(xb, w_in, ncols=n_main, col_scale=_q_scale(FOX_HEADS * hb, n_main),
                        out_dtype=BF16).reshape(b, seq, n_main)
            f_logit = matmul(xb, w_in[:, n_main:]).reshape(b, seq, FOX_HEADS)
            cum = forget_gate_cumsum(f_logit, b_f[i])
            pair = 2
            cum_rows = cum.transpose(0, 2, 1).reshape(b, FOX_HEADS // pair, pair, seq)
            o = forgetting_attention(hq, cum_rows, n_heads=FOX_HEADS, heads=pair).reshape(t, -1)
            mix = matmul(o, w_out_odd, lead=i)
        xf, xb = residual_layer_norm(xf, mix, ln_mix_g, ln_mix_b, layer)
        ffn = moe_ffn(xf, xb, router_w, router_bias, w_gate, w_up, w_down, layer)
        xf, xb = residual_layer_norm(xf, ffn, ln_ffn_g, ln_ffn_b, layer)
    return xf.reshape(b, seq, d)
```

```python
import functools

import jax
import jax.numpy as jnp
import numpy as np
from jax import lax
from jax.experimental import pallas as pl
from jax.experimental.pallas import tpu as pltpu

F32 = jnp.float32
BF16 = jnp.bfloat16

V7X_LANES = 128
V7X_VMEM_BYTES = 64 * 1024 * 1024
VMEM_LIMIT = 56 * 1024 * 1024

HEAD_DIM = 128
DSWA_HEADS = 16
DSWA_DILATIONS = (1, 4, 16)
BAND = 128
DSWA_QUERY_BLOCK = 256
GDN_HEADS = 16
GDN_CONV = 4
GDN_CHUNK = 128
GDN_INTERLEAVE = 4
FOX_HEADS = 32
ROPE_THETA = 500000.0
ROPE_DIM = HEAD_DIM // 4
N_EXPERTS = 32
N_GROUPS = 8
EXPERTS_PER_GROUP = 4
D_EXPERT = 768
MOE_ROWS = 512
LN_EPS = 1e-5
NORM_EPS = 1e-6
NEG_INF = -1e30
DEPTH = 2
DEEPNORM_ALPHA = (2 * DEPTH) ** 0.25
LOG2E = 1.4426950408889634
ATTN_Q_SCALE = HEAD_DIM ** -0.5 * LOG2E


def _params(*sem):
    return pltpu.CompilerParams(dimension_semantics=sem, vmem_limit_bytes=VMEM_LIMIT)


def _mm_kernel(x_ref, w_ref, s_ref, o_ref, *scratch):
    if scratch:
        wb_ref, = scratch

        @pl.when(pl.program_id(1) == 0)
        def _():
            wb_ref[...] = w_ref[...].astype(BF16)
    else:
        wb_ref = w_ref
    acc = jnp.dot(x_ref[...], wb_ref[...], preferred_element_type=F32)
    o_ref[...] = (acc * s_ref[...]).astype(o_ref.dtype)


def matmul(x, w, *, lead=None, col0=0, ncols=None, col_scale=None, out_dtype=F32, tm=1024, tn=512):
    m, k = x.shape
    n_total = w.shape[-1]
    ncols = n_total - col0 if ncols is None else ncols
    tn = min(tn, ncols)
    tm = min(tm, m)
    assert ncols % tn == 0 and col0 % tn == 0 and m % tm == 0
    cb = col0 // tn
    if lead is None:
        w_spec = pl.BlockSpec((k, tn), lambda j, i: (0, j + cb))
    else:
        w_spec = pl.BlockSpec((None, k, tn), lambda j, i: (lead, 0, j + cb))
    if col_scale is None:
        col_scale = jnp.ones((1, ncols), F32)
    return pl.pallas_call(
        _mm_kernel,
        out_shape=jax.ShapeDtypeStruct((m, ncols), out_dtype),
        grid=(ncols // tn, m // tm),
        in_specs=[pl.BlockSpec((tm, k), lambda j, i: (i, 0)), w_spec,
                  pl.BlockSpec((1, tn), lambda j, i: (0, j))],
        out_specs=pl.BlockSpec((tm, tn), lambda j, i: (i, j)),
        scratch_shapes=[] if w.dtype == BF16 else [pltpu.VMEM((k, tn), BF16)],
        compiler_params=_params("arbitrary", "arbitrary"),
        name="matmul",
    )(x, w, col_scale)


def _ln_kernel(x_ref, mix_ref, g_ref, b_ref, o_ref, ob_ref):
    y = DEEPNORM_ALPHA * x_ref[...] + mix_ref[...]
    mu = jnp.mean(y, axis=-1, keepdims=True)
    yc = y - mu
    var = jnp.mean(yc * yc, axis=-1, keepdims=True)
    out = yc * lax.rsqrt(var + LN_EPS) * g_ref[...] + b_ref[...]
    o_ref[...] = out
    ob_ref[...] = out.astype(BF16)


def residual_layer_norm(x, mix, g, b, layer, *, tm=256):
    m, d = x.shape
    tm = min(tm, m)
    row = pl.BlockSpec((tm, d), lambda i: (i, 0))
    par = pl.BlockSpec((None, 1, d), lambda i: (layer, 0, 0))
    return pl.pallas_call(
        _ln_kernel,
        out_shape=(jax.ShapeDtypeStruct((m, d), F32), jax.ShapeDtypeStruct((m, d), BF16)),
        grid=(m // tm,),
        in_specs=[row, row, par, par],
        out_specs=(row, row),
        compiler_params=_params("arbitrary"),
        name="residual_ln",
    )(x, mix, g.reshape(-1, 1, d), b.reshape(-1, 1, d))


def rotary_tables(seq):
    half = ROPE_DIM // 2
    inv_freq = ROPE_THETA ** (-jnp.arange(half, dtype=F32) / half)
    ang = jnp.arange(seq, dtype=F32)[:, None] * inv_freq[None, :]
    cos, sin = jnp.cos(ang), jnp.sin(ang)
    ones = jnp.ones((seq, HEAD_DIM - ROPE_DIM), F32)
    zeros = jnp.zeros((seq, HEAD_DIM - ROPE_DIM), F32)
    zh = jnp.zeros((seq, half), F32)
    c = jnp.concatenate([cos, cos, ones], axis=-1)
    s_lo = jnp.concatenate([-sin, zh, zeros], axis=-1)
    s_hi = jnp.concatenate([zh, sin, zeros], axis=-1)
    return c, s_lo, s_hi


def _dswa_kernel(q_ref, k_ref, v_ref, c_ref, slo_ref, shi_ref, o_ref, qs, ks, acc, ms, ls, *, seq):
    half = ROPE_DIM // 2
    rows = 256

    def prep(i, carry):
        sl = pl.ds(pl.multiple_of(i * rows, rows), rows)
        c, slo, shi = c_ref[sl, :], slo_ref[sl, :], shi_ref[sl, :]
        for src, dst in ((q_ref, qs), (k_ref, ks)):
            x = src[sl, :]
            dst[sl, :] = (x * c + pltpu.roll(x, HEAD_DIM - half, 1) * slo
                          + pltpu.roll(x, half, 1) * shi)
        acc[sl, :] = jnp.zeros((rows, HEAD_DIM), F32)
        ls[sl, :] = jnp.zeros((rows, HEAD_DIM), F32)
        ms[sl, :] = jnp.full((rows, HEAD_DIM), NEG_INF, F32)
        return carry

    lax.fori_loop(0, seq // rows, prep, 0)

    qb = DSWA_QUERY_BLOCK
    qi = lax.broadcasted_iota(jnp.int32, (qb, BAND + qb), 0)
    kj = lax.broadcasted_iota(jnp.int32, (qb, BAND + qb), 1)
    band_mask = (kj >= qi) & (kj <= qi + BAND)
    qi0 = lax.broadcasted_iota(jnp.int32, (qb, qb), 0)
    kj0 = lax.broadcasted_iota(jnp.int32, (qb, qb), 1)
    first_mask = (kj0 <= qi0) & (kj0 >= qi0 - BAND)

    def scores(q, k2):
        return lax.dot_general(q.astype(BF16), k2.astype(BF16), (((1,), (1,)), ((), ())),
                               preferred_element_type=F32)

    def process(cur, prev):
        if prev is None:
            k2, v2, mask = ks[cur, :], v_ref[cur, :], first_mask
        else:
            k2 = jnp.concatenate([ks[prev, :], ks[cur, :]], axis=0)
            v2 = jnp.concatenate([v_ref[prev, :], v_ref[cur, :]], axis=0)
            mask = band_mask
        s = jnp.where(mask, scores(qs[cur, :], k2), NEG_INF)
        m_old = ms[cur, :]
        m_new = jnp.maximum(m_old[:, :1], jnp.max(s, axis=1, keepdims=True))
        alpha = jnp.exp2(m_old - m_new)
        p = jnp.exp2(s - m_new)
        ls[cur, :] = alpha * ls[cur, :] + jnp.sum(p, axis=1, keepdims=True)
        acc[cur, :] = alpha * acc[cur, :] + jnp.dot(p.astype(BF16), v2.astype(BF16),
                                                    preferred_element_type=F32)
        ms[cur, :] = jnp.broadcast_to(m_new, (qb, HEAD_DIM))

    for d in DSWA_DILATIONS:
        n_blk = seq // (d * qb)
        step = d * qb
        for r in range(d):
            def rows_of(start, size):
                return pl.ds(start, size) if d == 1 else pl.ds(start, size, stride=d)

            def body(n, carry):
                first = r + n * step
                process(rows_of(first, qb), rows_of(first - d * BAND, BAND))
                return carry

            process(rows_of(r, qb), None)
            lax.fori_loop(1, n_blk, body, 0, unroll=2)

    def fin(i, carry):
        sl = pl.ds(pl.multiple_of(i * rows, rows), rows)
        o_ref[sl, :] = (acc[sl, :] / ls[sl, :]).astype(o_ref.dtype)
        return carry

    lax.fori_loop(0, seq // rows, fin, 0)


def dilated_mixture_attention(h, tables, *, q_col, k_col, v_col, n_heads):
    b, seq, _ = h.shape
    assert seq % (max(DSWA_DILATIONS) * DSWA_QUERY_BLOCK) == 0
    col = lambda c0: pl.BlockSpec((None, seq, HEAD_DIM), lambda bi, hi: (bi, 0, c0 + hi))
    tab = pl.BlockSpec((seq, HEAD_DIM), lambda bi, hi: (0, 0))
    vm = lambda: pltpu.VMEM((seq, HEAD_DIM), F32)
    return pl.pallas_call(
        functools.partial(_dswa_kernel, seq=seq),
        out_shape=jax.ShapeDtypeStruct((b, seq, n_heads * HEAD_DIM), BF16),
        grid=(b, n_heads),
        in_specs=[col(q_col), col(k_col), col(v_col), tab, tab, tab],
        out_specs=pl.BlockSpec((None, seq, HEAD_DIM), lambda bi, hi: (bi, 0, hi)),
        scratch_shapes=[vm(), vm(), vm(), vm(), vm()],
        compiler_params=_params("arbitrary", "arbitrary"),
        name="dswa",
    )(h, h, h, *tables)


def _nt(a, b):
    return lax.dot_general(a.astype(BF16), b.astype(BF16), (((1,), (1,)), ((), ())),
                           preferred_element_type=F32)


def _nn(a, b):
    return jnp.dot(a.astype(BF16), b.astype(BF16), preferred_element_type=F32)


def _gdn_kernel(alog_ref, dtb_ref, q_ref, k_ref, v_ref, gate_ref, cq_ref, ck_ref, cv_ref, ab_ref,
                nw_ref, o_ref, qn, kn, vn, gcs, gct, bts, us, wq, ak, *, seq):
    hi = pl.program_id(1)
    c = GDN_CHUNK
    rows = 2 * c
    halo = 8
    neg_a = -jnp.exp(jnp.full((1, 1), alog_ref[hi], F32))
    dt_b = dtb_ref[hi]
    row_i = lax.broadcasted_iota(jnp.int32, (c, c), 0)
    col_i = lax.broadcasted_iota(jnp.int32, (c, c), 1)
    tri_incl = (col_i <= row_i).astype(F32)
    eye = (col_i == row_i).astype(F32)
    lane32 = lax.broadcasted_iota(jnp.int32, (rows, 2 * GDN_HEADS), 1)

    def conv_silu(xh, w_ref):
        y = xh[halo:, :] * w_ref[GDN_CONV - 1:GDN_CONV, :]
        for j in range(1, GDN_CONV):
            y = y + pltpu.roll(xh, j, 0)[halo:, :] * w_ref[GDN_CONV - 1 - j:GDN_CONV - j, :]
        return y * (1.0 / (1.0 + jnp.exp(-y)))

    def l2n(t):
        return t * lax.rsqrt(jnp.sum(t * t, axis=-1, keepdims=True) + NORM_EPS)

    def prep(i, first):
        r0 = pl.multiple_of(i * rows, rows)
        sl = pl.ds(r0, rows)

        def with_halo(ref):
            if first:
                return jnp.concatenate([jnp.zeros((halo, HEAD_DIM), F32), ref[sl, :]], axis=0)
            return ref[pl.ds(pl.multiple_of(r0 - halo, halo), rows + halo), :]

        qn[sl, :] = l2n(conv_silu(with_halo(q_ref), cq_ref)) * (HEAD_DIM ** -0.5)
        kn[sl, :] = l2n(conv_silu(with_halo(k_ref), ck_ref))
        vn[sl, :] = conv_silu(with_halo(v_ref), cv_ref)
        ab = ab_ref[sl, :]
        b_logit = jnp.sum(jnp.where(lane32 == hi, ab, 0.0), axis=1, keepdims=True)
        a_logit = jnp.sum(jnp.where(lane32 == hi + GDN_HEADS, ab, 0.0), axis=1, keepdims=True)
        z = a_logit + dt_b
        g = neg_a * (jnp.maximum(z, 0.0) + jnp.log(1.0 + jnp.exp(-jnp.abs(z))))
        bts[sl, :] = jnp.broadcast_to(1.0 / (1.0 + jnp.exp(-b_logit)), (rows, HEAD_DIM))
        gb = jnp.broadcast_to(g, (rows, HEAD_DIM))
        for half in range(rows // c):
            hs = pl.ds(pl.multiple_of(r0 + half * c, c), c)
            gc = jnp.dot(tri_incl, gb[half * c:(half + 1) * c, :], precision=lax.Precision.HIGHEST,
                         preferred_element_type=F32)
            gcs[hs, :] = gc
            gct[hs, :] = gc.T

    prep(0, True)
    lax.fori_loop(1, seq // rows, lambda i, carry: (prep(i, False), carry)[1], 0)

    hd = HEAD_DIM
    left = lax.broadcasted_iota(jnp.int32, (c, 2 * c), 1) < c
    causal2 = jnp.concatenate([col_i <= row_i, col_i <= row_i], axis=1)
    strict2 = jnp.concatenate([col_i < row_i, col_i < row_i], axis=1)
    eye2 = jnp.concatenate([eye, eye], axis=1)

    def pair_diag(x2):
        return jnp.concatenate([jnp.where(left, x2, 0.0), jnp.where(left, 0.0, x2)], axis=0)

    def independent(qa, ka, va, ba, gca, gra, qb, kb_, vb, bb, gcb, grb):
        lanes = lambda xa, xb: jnp.concatenate([xa, xb], axis=1)
        q2, k2, beta2, gc2 = lanes(qa, qb), lanes(ka, kb_), lanes(ba, bb), lanes(gca, gcb)
        decay2 = jnp.exp(jnp.where(causal2, gc2 - lanes(gra, grb), -jnp.inf))
        kbeta2 = k2 * beta2
        kq = _nt(jnp.concatenate([kbeta2, q2], axis=0), pair_diag(k2))
        nm = kq[:c, :] * jnp.where(strict2, decay2, 0.0)
        attn2 = kq[c:, :] * decay2
        inv = eye2 - nm
        pw = _nn(nm, pair_diag(nm))
        for lvl in range(6):
            if lvl < 5:
                both = _nn(jnp.concatenate([inv, pw], axis=0), pair_diag(pw))
                inv, pw = inv + both[:c, :], both[c:, :]
            else:
                inv = inv + _nn(inv, pair_diag(pw))
        egc2 = jnp.exp(gc2)
        vbeta2, kbe2, qd2 = lanes(va, vb) * beta2, kbeta2 * egc2, q2 * egc2
        rhs_a = jnp.concatenate([vbeta2[:, :hd], kbe2[:, :hd]], axis=1)
        rhs_b = jnp.concatenate([vbeta2[:, hd:], kbe2[:, hd:]], axis=1)
        zero = jnp.zeros_like(rhs_a)
        uw = _nn(inv, jnp.concatenate([jnp.concatenate([rhs_a, zero], axis=1),
                                       jnp.concatenate([zero, rhs_b], axis=1)], axis=0))
        out = []
        for h, (k, gc) in enumerate(((ka, gca), (kb_, gcb))):
            lo = h * 2 * hd
            kdt = (k * jnp.exp(gc[c - 1:c, :] - gc)).T
            out.append((uw[:, lo:lo + hd],
                        jnp.concatenate([uw[:, lo + hd:lo + 2 * hd], qd2[:, h * hd:(h + 1) * hd]],
                                        axis=0).astype(BF16),
                        jnp.concatenate([attn2[:, h * hd:(h + 1) * hd], kdt], axis=0).astype(BF16)))
        return out

    group = GDN_INTERLEAVE
    assert group % 2 == 0

    def independent_group(i, carry):
        sls = [pl.ds(pl.multiple_of((i * group + u) * c, c), c) for u in range(group)]
        s2s = [pl.ds(pl.multiple_of((i * group + u) * (2 * c), 2 * c), 2 * c) for u in range(group)]
        loaded = [tuple(ref[sl, :] for ref in (qn, kn, vn, bts, gcs, gct)) for sl in sls]
        results = []
        for u in range(0, group, 2):
            results += independent(*loaded[u], *loaded[u + 1])
        for sl, s2, (u_new, wq_new, ak_new) in zip(sls, s2s, results):
            us[sl, :] = u_new
            wq[s2, :] = wq_new
            ak[s2, :] = ak_new
        return carry

    lax.fori_loop(0, seq // (c * group), independent_group, 0)

    def scan(n, state):
        sl = pl.ds(pl.multiple_of(n * c, c), c)
        s2 = pl.ds(pl.multiple_of(n * (2 * c), 2 * c), 2 * c)
        ws_qs = _nn(wq[s2, :], state)
        v_new = us[sl, :] - ws_qs[:c, :]
        av_kv = _nn(ak[s2, :], v_new)
        o = ws_qs[c:, :] + av_kv[:c, :]
        last = jnp.exp(gcs[pl.ds(n * c + c - 1, 1), :])
        o = o * lax.rsqrt(jnp.mean(o * o, axis=-1, keepdims=True) + NORM_EPS) * nw_ref[...]
        gate = gate_ref[sl, :]
        o_ref[sl, :] = (o * (gate * (1.0 / (1.0 + jnp.exp(-gate))))).astype(o_ref.dtype)
        return state * last + av_kv[c:, :]

    lax.fori_loop(0, seq // c, scan, jnp.zeros((HEAD_DIM, HEAD_DIM), F32))


def gated_delta_net(h, ab, conv_w, a_log, dt_bias, norm_w, *, q_col, k_col, v_col, gate_col):
    b, seq, _ = h.shape
    nh = GDN_HEADS
    assert seq % (max(2, GDN_INTERLEAVE) * GDN_CHUNK) == 0
    col = lambda c0: pl.BlockSpec((None, seq, HEAD_DIM), lambda bi, hi, *_: (bi, 0, c0 + hi))
    cw = lambda c0: pl.BlockSpec((GDN_CONV, HEAD_DIM), lambda bi, hi, *_: (0, c0 + hi))
    f32s = lambda: pltpu.VMEM((seq, HEAD_DIM), F32)
    b16s = lambda: pltpu.VMEM((2 * seq, HEAD_DIM), BF16)
    grid_spec = pltpu.PrefetchScalarGridSpec(
        num_scalar_prefetch=2,
        grid=(b, nh),
        in_specs=[col(q_col), col(k_col), col(v_col), col(gate_col), cw(0), cw(nh), cw(2 * nh),
                  pl.BlockSpec((None, seq, 2 * nh), lambda bi, hi, *_: (bi, 0, 0)),
                  pl.BlockSpec((1, HEAD_DIM), lambda bi, hi, *_: (0, 0))],
        out_specs=pl.BlockSpec((None, seq, HEAD_DIM), lambda bi, hi, *_: (bi, 0, hi)),
        scratch_shapes=[f32s(), f32s(), f32s(), f32s(), f32s(), f32s(), f32s(), b16s(), b16s()],
    )
    return pl.pallas_call(
        functools.partial(_gdn_kernel, seq=seq),
        out_shape=jax.ShapeDtypeStruct((b, seq, nh * HEAD_DIM), BF16),
        grid_spec=grid_spec,
        compiler_params=_params("arbitrary", "arbitrary"),
        name="gdn",
    )(a_log, dt_bias, h, h, h, h, conv_w, conv_w, conv_w, ab, norm_w.reshape(1, HEAD_DIM))


def _fox_cum_kernel(fl_ref, bf_ref, o_ref, *, seq):
    c = 128
    tri_incl = (lax.broadcasted_iota(jnp.int32, (c, c), 1)
                <= lax.broadcasted_iota(jnp.int32, (c, c), 0)).astype(F32)

    def body(n, carry):
        sl = pl.ds(pl.multiple_of(n * c, c), c)
        z = fl_ref[sl, :] + bf_ref[...]
        log_f = -(jnp.maximum(-z, 0.0) + jnp.log(1.0 + jnp.exp(-jnp.abs(z))))
        cs = jnp.dot(tri_incl, log_f, precision=lax.Precision.HIGHEST,
                     preferred_element_type=F32) + carry
        o_ref[sl, :] = cs * LOG2E
        return cs[c - 1:c, :]

    lax.fori_loop(0, seq // c, body, jnp.zeros((1, fl_ref.shape[-1]), F32))


def forget_gate_cumsum(f_logit, b_f):
    b, seq, nh = f_logit.shape
    blk = pl.BlockSpec((None, seq, nh), lambda bi: (bi, 0, 0))
    return pl.pallas_call(
        functools.partial(_fox_cum_kernel, seq=seq),
        out_shape=jax.ShapeDtypeStruct((b, seq, nh), F32),
        grid=(b,),
        in_specs=[blk, pl.BlockSpec((1, nh), lambda bi: (0, 0))],
        out_specs=blk,
        compiler_params=_params("arbitrary"),
        name="fox_cumsum",
    )(f_logit, b_f.reshape(1, nh))


def _fox_kernel(q_ref, k_ref, v_ref, cum_ref, o_ref, *, blk, heads):
    qi = pl.program_id(2)
    hd = HEAD_DIM
    ri = lax.broadcasted_iota(jnp.int32, (blk, blk), 0)
    ci = lax.broadcasted_iota(jnp.int32, (blk, blk), 1)
    eye = (lax.broadcasted_iota(jnp.int32, (hd, hd), 0)
           == lax.broadcasted_iota(jnp.int32, (hd, hd), 1))
    qs = pl.ds(pl.multiple_of(qi * blk, blk), blk)

    def as_column(row):
        return jnp.concatenate(
            [jnp.sum(jnp.where(eye, row[:, c * hd:(c + 1) * hd], 0.0), axis=1, keepdims=True)
             for c in range(blk // hd)], axis=0)

    cq = [as_column(cum_ref[g:g + 1, qs]) for g in range(heads)]
    q = [q_ref[:, g * hd:(g + 1) * hd] for g in range(heads)]

    def step(j, carry, diagonal):
        ks = pl.ds(pl.multiple_of(j * blk, blk), blk)
        out = []
        for g in range(heads):
            m, l, acc = carry[g]
            s = _nt(q[g], k_ref[ks, g * hd:(g + 1) * hd]) + (cq[g] - cum_ref[g:g + 1, ks])
            if diagonal:
                s = jnp.where(ri >= ci, s, NEG_INF)
            m_new = jnp.maximum(m, jnp.max(s, axis=1, keepdims=True))
            alpha = jnp.exp2(m - m_new)
            p = jnp.exp2(s - m_new)
            l = alpha * l + jnp.sum(p, axis=1, keepdims=True)
            acc = alpha * acc + jnp.dot(p.astype(BF16), v_ref[ks, g * hd:(g + 1) * hd],
                                        preferred_element_type=F32)
            out.append((m_new, l, acc))
        return tuple(out)

    init = tuple((jnp.full((blk, 1), NEG_INF, F32), jnp.zeros((blk, 1), F32),
                  jnp.zeros((blk, hd), F32)) for _ in range(heads))
    carry = lax.fori_loop(0, qi, lambda j, cr: step(j, cr, False), init)
    carry = step(qi, carry, True)
    for g in range(heads):
        _, l, acc = carry[g]
        o_ref[:, g * hd:(g + 1) * hd] = (acc / l).astype(o_ref.dtype)


def forgetting_attention(hq, cum_rows, *, n_heads, blk=512, heads=2):
    b, seq, _ = hq.shape
    assert seq % blk == 0 and n_heads % heads == 0
    w = heads * HEAD_DIM
    ng = n_heads // heads
    kv = lambda c0: pl.BlockSpec((None, seq, w), lambda bi, hi, qi: (bi, 0, c0 + hi))
    return pl.pallas_call(
        functools.partial(_fox_kernel, blk=blk, heads=heads),
        out_shape=jax.ShapeDtypeStruct((b, seq, n_heads * HEAD_DIM), BF16),
        grid=(b, ng, seq // blk),
        in_specs=[pl.BlockSpec((None, blk, w), lambda bi, hi, qi: (bi, qi, hi)),
                  kv(ng), kv(2 * ng),
                  pl.BlockSpec((None, None, heads, seq), lambda bi, hi, qi: (bi, hi, 0, 0))],
        out_specs=pl.BlockSpec((None, blk, w), lambda bi, hi, qi: (bi, qi, hi)),
        compiler_params=_params("arbitrary", "arbitrary", "arbitrary"),
        name="fox",
    )(hq, hq, hq, cum_rows)


def _router_kernel(x_ref, rw_ref, rb_ref, idx_ref, gate_ref):
    logits = lax.dot_general(rw_ref[...], x_ref[...], (((1,), (1,)), ((), ())),
                             precision=lax.Precision.HIGHEST, preferred_element_type=F32)
    score = 1.0 / (1.0 + jnp.exp(-logits))
    sel = score + rb_ref[...]
    sel_r = [sel[e:e + 1, :] for e in range(N_EXPERTS)]
    score_r = [score[e:e + 1, :] for e in range(N_EXPERTS)]

    def top2_sum(a, b, c, d):
        hi1, lo1, hi2, lo2 = jnp.maximum(a, b), jnp.minimum(a, b), jnp.maximum(c, d), jnp.minimum(c, d)
        return jnp.maximum(hi1, hi2) + jnp.maximum(jnp.minimum(hi1, hi2), jnp.maximum(lo1, lo2))

    epg = EXPERTS_PER_GROUP
    best = top2_sum(*sel_r[:epg])
    grp = jnp.zeros_like(best, dtype=jnp.int32)
    for g in range(1, N_GROUPS):
        cand = top2_sum(*sel_r[g * epg:(g + 1) * epg])
        take = cand > best
        best = jnp.where(take, cand, best)
        grp = jnp.where(take, g, grp)
    in_sel, in_score = [], []
    for j in range(epg):
        s_j, u_j = sel_r[j], score_r[j]
        for g in range(1, N_GROUPS):
            s_j = jnp.where(grp == g, sel_r[g * epg + j], s_j)
            u_j = jnp.where(grp == g, score_r[g * epg + j], u_j)
        in_sel.append(s_j)
        in_score.append(u_j)

    def arg_best(vals):
        bv, bi = vals[0], jnp.zeros_like(grp)
        for j in range(1, epg):
            take = vals[j] > bv
            bv = jnp.where(take, vals[j], bv)
            bi = jnp.where(take, j, bi)
        return bi

    i1 = arg_best(in_sel)
    i2 = arg_best([jnp.where(i1 == j, -jnp.inf, in_sel[j]) for j in range(epg)])

    def pick(vals, i):
        out = vals[0]
        for j in range(1, epg):
            out = jnp.where(i == j, vals[j], out)
        return out

    g1, g2 = pick(in_score, i1), pick(in_score, i2)
    tot = g1 + g2
    idx_ref[0:1, :] = grp * epg + i1
    idx_ref[1:2, :] = grp * epg + i2
    gate_ref[0:1, :] = g1 / tot
    gate_ref[1:2, :] = g2 / tot


def route(x, router_w, router_bias, *, tm=512):
    t, d = x.shape
    tm = min(tm, t)
    out = pl.BlockSpec((2, tm), lambda i: (0, i))
    return pl.pallas_call(
        _router_kernel,
        out_shape=(jax.ShapeDtypeStruct((2, t), jnp.int32), jax.ShapeDtypeStruct((2, t), F32)),
        grid=(t // tm,),
        in_specs=[pl.BlockSpec((tm, d), lambda i: (i, 0)),
                  pl.BlockSpec((N_EXPERTS, d), lambda i: (0, 0)),
                  pl.BlockSpec((N_EXPERTS, 1), lambda i: (0, 0))],
        out_specs=(out, out),
        compiler_params=_params("arbitrary"),
        name="router",
    )(x, router_w.T, router_bias.reshape(N_EXPERTS, 1))


def _moe_up_kernel(be_ref, first_ref, nused_ref, x_ref, wg_ref, wu_ref, o_ref, wgb, wub):
    i = pl.program_id(1)

    @pl.when(i < nused_ref[0])
    def _():
        @pl.when(first_ref[i] == 1)
        def _():
            wgb[...] = wg_ref[...].astype(BF16)
            wub[...] = wu_ref[...].astype(BF16)

        xs = x_ref[...]
        a = jnp.dot(xs, wgb[...], preferred_element_type=F32)
        u = jnp.dot(xs, wub[...], preferred_element_type=F32)
        o_ref[...] = (a * (1.0 / (1.0 + jnp.exp(-a))) * u).astype(o_ref.dtype)

    @pl.when(i >= nused_ref[0])
    def _():
        o_ref[...] = jnp.zeros_like(o_ref)


def _moe_down_kernel(be_ref, first_ref, nused_ref, h_ref, wd_ref, o_ref, wdb):
    i = pl.program_id(1)

    @pl.when(i < nused_ref[0])
    def _():
        @pl.when(first_ref[i] == 1)
        def _():
            wdb[...] = wd_ref[...].astype(BF16)

        o_ref[...] = jnp.dot(h_ref[...], wdb[...], preferred_element_type=F32)

    @pl.when(i >= nused_ref[0])
    def _():
        o_ref[...] = jnp.zeros_like(o_ref)


def expert_ffn(xg, block_expert, block_first, n_used, w_gate, w_up, w_down, layer, *, tn_up=256,
               tn_down=1024):
    r, d = xg.shape
    de = w_gate.shape[-1]
    nb = r // MOE_ROWS
    w_in = pl.BlockSpec((None, None, d, tn_up), lambda j, i, be, fi, nu: (layer, be[i], 0, j))
    hid = pl.pallas_call(
        _moe_up_kernel,
        out_shape=jax.ShapeDtypeStruct((r, de), BF16),
        grid_spec=pltpu.PrefetchScalarGridSpec(
            num_scalar_prefetch=3,
            grid=(de // tn_up, nb),
            in_specs=[pl.BlockSpec((MOE_ROWS, d), lambda j, i, *_: (i, 0)), w_in, w_in],
            out_specs=pl.BlockSpec((MOE_ROWS, tn_up), lambda j, i, *_: (i, j)),
            scratch_shapes=[pltpu.VMEM((d, tn_up), BF16), pltpu.VMEM((d, tn_up), BF16)],
        ),
        compiler_params=_params("arbitrary", "arbitrary"),
        name="moe_up",
    )(block_expert, block_first, n_used, xg, w_gate, w_up)
    return pl.pallas_call(
        _moe_down_kernel,
        out_shape=jax.ShapeDtypeStruct((r, d), F32),
        grid_spec=pltpu.PrefetchScalarGridSpec(
            num_scalar_prefetch=3,
            grid=(d // tn_down, nb),
            in_specs=[pl.BlockSpec((MOE_ROWS, de), lambda j, i, *_: (i, 0)),
                      pl.BlockSpec((None, None, de, tn_down),
                                   lambda j, i, be, fi, nu: (layer, be[i], 0, j))],
            out_specs=pl.BlockSpec((MOE_ROWS, tn_down), lambda j, i, *_: (i, j)),
            scratch_shapes=[pltpu.VMEM((de, tn_down), BF16)],
        ),
        compiler_params=_params("arbitrary", "arbitrary"),
        name="moe_down",
    )(block_expert, block_first, n_used, hid, w_down)


DMA_PRIORITIES = 2


def _gather_kernel(idx_ref, nused_ref, src_ref, o_ref, sem, *, rows):
    i = pl.program_id(0)
    base = i * rows
    burst = 8
    assert rows % burst == 0

    def row_copy(r, src_row):
        return pltpu.make_async_copy(src_ref.at[pl.ds(src_row, 1)], o_ref.at[pl.ds(r, 1)], sem)

    def issue(g, carry):
        for u in range(burst):
            r = g * burst + u
            row_copy(r, idx_ref[base + r]).start(priority=u % DMA_PRIORITIES)
        return carry

    def drain(g, carry):
        for u in range(burst):
            row_copy(g * burst + u, 0).wait()
        return carry

    @pl.when(i < nused_ref[0])
    def _():
        lax.fori_loop(0, rows // burst, issue, 0)
        lax.fori_loop(0, rows // burst, drain, 0)

    @pl.when(i >= nused_ref[0])
    def _():
        o_ref[...] = jnp.zeros_like(o_ref)


def gather_rows(src, idx, n_used, *, rows=MOE_ROWS):
    m = idx.shape[0]
    assert m % rows == 0
    _, s, lanes = src.shape
    return pl.pallas_call(
        functools.partial(_gather_kernel, rows=rows),
        out_shape=jax.ShapeDtypeStruct((m, s, lanes), src.dtype),
        grid_spec=pltpu.PrefetchScalarGridSpec(
            num_scalar_prefetch=2,
            grid=(m // rows,),
            in_specs=[pl.BlockSpec(memory_space=pl.ANY)],
            out_specs=pl.BlockSpec((rows, s, lanes), lambda i, *_: (i, 0, 0)),
            scratch_shapes=[pltpu.SemaphoreType.DMA(())],
        ),
        compiler_params=_params("arbitrary"),
        name="gather_rows",
    )(idx, n_used, src)


def moe_ffn(x, xb, router_w, router_bias, w_gate, w_up, w_down, layer):
    t, d = x.shape
    idx, gate = route(x, router_w, router_bias)
    flat_e = idx.reshape(-1)
    n_assign = flat_e.shape[0]
    order = jnp.argsort(flat_e).astype(jnp.int32)
    e_sorted = flat_e[order]
    bounds = jnp.searchsorted(e_sorted, jnp.arange(N_EXPERTS + 1, dtype=jnp.int32)).astype(jnp.int32)
    start, counts = bounds[:-1], bounds[1:] - bounds[:-1]
    padded = (counts + MOE_ROWS - 1) // MOE_ROWS * MOE_ROWS
    pad_end = jnp.cumsum(padded)
    pad_start = pad_end - padded
    dest_sorted = pad_start[e_sorted] + jnp.arange(n_assign, dtype=jnp.int32) - start[e_sorted]
    n_blocks = -(-(n_assign + N_EXPERTS * (MOE_ROWS - 1)) // MOE_ROWS)
    rows = n_blocks * MOE_ROWS
    blk_start = jnp.arange(n_blocks, dtype=jnp.int32) * MOE_ROWS
    block_expert = jnp.minimum(jnp.searchsorted(pad_end, blk_start, side='right'),
                               N_EXPERTS - 1).astype(jnp.int32)
    n_used = (pad_end[-1] // MOE_ROWS).astype(jnp.int32).reshape(1)
    block_first = jnp.concatenate([jnp.ones((1,), jnp.int32),
                                   (block_expert[1:] != block_expert[:-1]).astype(jnp.int32)])
    row = jnp.arange(rows, dtype=jnp.int32)
    row_e = jnp.repeat(block_expert, MOE_ROWS)
    rank = row - pad_start[row_e]
    row_tok = jnp.where(rank < counts[row_e],
                        order[jnp.minimum(start[row_e] + rank, n_assign - 1)] % t, row % t)
    pos = dest_sorted[jnp.argsort(order)]
    xg = gather_rows(xb.reshape(t, d // V7X_LANES, V7X_LANES), row_tok, n_used).reshape(rows, d)
    yg = expert_ffn(xg, block_expert, block_first, n_used, w_gate, w_up, w_down, layer)
    return gate[0][:, None] * yg[pos[:t]] + gate[1][:, None] * yg[pos[t:]]


def _q_scale(n_q, n_total):
    return jnp.concatenate([jnp.full((1, n_q), ATTN_Q_SCALE, F32),
                            jnp.ones((1, n_total - n_q), F32)], axis=1)


def kernel(x, w_in_even, conv_w, a_log, dt_bias, gdn_norm_w, w_out_even, w_in_odd, b_f, w_out_odd, ln_mix_g, ln_mix_b, ln_ffn_g, ln_ffn_b, w_gate, w_up, w_down, router_w, router_bias):
    b, seq, d = x.shape
    t = b * seq
    xf = x.reshape(t, d)
    xb = xf.astype(BF16)
    tables = rotary_tables(seq)
    hb = HEAD_DIM
    for layer in range(DEPTH):
        i = layer // 2
        if layer % 2 == 0:
            wa, wqkv, wg = DSWA_HEADS * hb, 3 * GDN_HEADS * hb, GDN_HEADS * hb
            n_main = 3 * wa + wqkv + wg
            w_in = w_in_even[i].astype(BF16)
            h = matmul(xb, w_in, ncols=n_main, col_scale=_q_scale(wa, n_main)).reshape(b, seq, n_main)
            ab = matmul(xb, w_in[:, n_main:]).reshape(b, seq, 2 * GDN_HEADS)
            o_a = dilated_mixture_attention(h, tables, q_col=0, k_col=DSWA_HEADS,
                                            v_col=2 * DSWA_HEADS, n_heads=DSWA_HEADS)
            c0 = 3 * DSWA_HEADS
            o_b = gated_delta_net(h, ab, conv_w[i], a_log[i], dt_bias[i], gdn_norm_w[i],
                                  q_col=c0, k_col=c0 + GDN_HEADS, v_col=c0 + 2 * GDN_HEADS,
                                  gate_col=c0 + 3 * GDN_HEADS)
            o = jnp.concatenate([o_a, o_b], axis=-1).reshape(t, -1)
            mix = matmul(o, w_out_even, lead=i)
        else:
            n_main = 3 * FOX_HEADS * hb
            w_in = w_in_odd[i].astype(BF16)
            hq = matmul(xb, w_in, ncols=n_main, col_scale=_q_scale(FOX_HEADS * hb, n_main),
                        out_dtype=BF16).reshape(b, seq, n_main)
            f_logit = matmul(xb, w_in[:, n_main:]).reshape(b, seq, FOX_HEADS)
            cum = forget_gate_cumsum(f_logit, b_f[i])
            pair = 2
            cum_rows = cum.transpose(0, 2, 1).reshape(b, FOX_HEADS // pair, pair, seq)
            o = forgetting_attention(hq, cum_rows, n_heads=FOX_HEADS, heads=pair).reshape(t, -1)
            mix = matmul(o, w_out_odd, lead=i)
        xf, xb = residual_layer_norm(xf, mix, ln_mix_g, ln_mix_b, layer)
        ffn = moe_ffn(xf, xb, router_w, router_bias, w_gate, w_up, w_down, layer)
        xf, xb = residual_layer_norm(xf, ffn, ln_ffn_g, ln_ffn_b, layer)
    return xf.reshape(b, seq, d)
```

```python
import functools

import jax
import jax.numpy as jnp
import numpy as np
from jax import lax
from jax.experimental import pallas as pl
from jax.experimental.pallas import tpu as pltpu

F32 = jnp.float32
BF16 = jnp.bfloat16

V7X_LANES = 128
V7X_VMEM_BYTES = 64 * 1024 * 1024
VMEM_LIMIT = 56 * 1024 * 1024

HEAD_DIM = 128
DSWA_HEADS = 16
DSWA_DILATIONS = (1, 4, 16)
BAND = 128
DSWA_QUERY_BLOCK = 256
GDN_HEADS = 16
GDN_CONV = 4
GDN_CHUNK = 128
GDN_INTERLEAVE = 4
FOX_HEADS = 32
ROPE_THETA = 500000.0
ROPE_DIM = HEAD_DIM // 4
N_EXPERTS = 32
N_GROUPS = 8
EXPERTS_PER_GROUP = 4
D_EXPERT = 768
MOE_ROWS = 512
LN_EPS = 1e-5
NORM_EPS = 1e-6
NEG_INF = -1e30
DEPTH = 2
DEEPNORM_ALPHA = (2 * DEPTH) ** 0.25
LOG2E = 1.4426950408889634
ATTN_Q_SCALE = HEAD_DIM ** -0.5 * LOG2E


def _params(*sem):
    return pltpu.CompilerParams(dimension_semantics=sem, vmem_limit_bytes=VMEM_LIMIT)


def _mm_kernel(x_ref, w_ref, s_ref, o_ref, *scratch):
    if scratch:
        wb_ref, = scratch

        @pl.when(pl.program_id(1) == 0)
        def _():
            wb_ref[...] = w_ref[...].astype(BF16)
    else:
        wb_ref = w_ref
    acc = jnp.dot(x_ref[...], wb_ref[...], preferred_element_type=F32)
    o_ref[...] = (acc * s_ref[...]).astype(o_ref.dtype)


def matmul(x, w, *, lead=None, col0=0, ncols=None, col_scale=None, out_dtype=F32, tm=1024, tn=512):
    m, k = x.shape
    n_total = w.shape[-1]
    ncols = n_total - col0 if ncols is None else ncols
    tn = min(tn, ncols)
    tm = min(tm, m)
    assert ncols % tn == 0 and col0 % tn == 0 and m % tm == 0
    cb = col0 // tn
    if lead is None:
        w_spec = pl.BlockSpec((k, tn), lambda j, i: (0, j + cb))
    else:
        w_spec = pl.BlockSpec((None, k, tn), lambda j, i: (lead, 0, j + cb))
    if col_scale is None:
        col_scale = jnp.ones((1, ncols), F32)
    return pl.pallas_call(
        _mm_kernel,
        out_shape=jax.ShapeDtypeStruct((m, ncols), out_dtype),
        grid=(ncols // tn, m // tm),
        in_specs=[pl.BlockSpec((tm, k), lambda j, i: (i, 0)), w_spec,
                  pl.BlockSpec((1, tn), lambda j, i: (0, j))],
        out_specs=pl.BlockSpec((tm, tn), lambda j, i: (i, j)),
        scratch_shapes=[] if w.dtype == BF16 else [pltpu.VMEM((k, tn), BF16)],
        compiler_params=_params("arbitrary", "arbitrary"),
        name="matmul",
    )(x, w, col_scale)


ROW_STRIDE = 40


def _store_token_rows(dst_ref, val):
    n, d = val.shape
    for c in range(ROW_STRIDE):
        chunk = val[:, c * V7X_LANES:(c + 1) * V7X_LANES] if c < d // V7X_LANES else jnp.zeros(
            (n, V7X_LANES), F32)
        dst_ref[pl.ds(c, n, stride=ROW_STRIDE), :] = chunk


def _token_chunk(src_ref, c, n):
    return src_ref[pl.ds(c, n, stride=ROW_STRIDE), :]


def _layer_norm_rows(y, g_ref, b_ref):
    mu = jnp.mean(y, axis=-1, keepdims=True)
    yc = y - mu
    var = jnp.mean(yc * yc, axis=-1, keepdims=True)
    return yc * lax.rsqrt(var + LN_EPS) * g_ref[...] + b_ref[...]


def _ln_kernel(x_ref, mix_ref, g_ref, b_ref, o_ref, ob_ref, *maybe_rows_ref):
    out = _layer_norm_rows(DEEPNORM_ALPHA * x_ref[...] + mix_ref[...], g_ref, b_ref)
    o_ref[...] = out
    ob_ref[...] = out.astype(BF16)
    for rows_ref in maybe_rows_ref:
        _store_token_rows(rows_ref, out)


def residual_layer_norm(x, mix, g, b, layer, *, token_rows=False, tm=256):
    m, d = x.shape
    tm = min(tm, m)
    row = pl.BlockSpec((tm, d), lambda i: (i, 0))
    par = pl.BlockSpec((None, 1, d), lambda i: (layer, 0, 0))
    out_shape = [jax.ShapeDtypeStruct((m, d), F32), jax.ShapeDtypeStruct((m, d), BF16)]
    out_specs = [row, row]
    if token_rows:
        out_shape.append(jax.ShapeDtypeStruct((m * ROW_STRIDE, V7X_LANES), F32))
        out_specs.append(pl.BlockSpec((tm * ROW_STRIDE, V7X_LANES), lambda i: (i, 0)))
    return pl.pallas_call(
        _ln_kernel,
        out_shape=tuple(out_shape),
        grid=(m // tm,),
        in_specs=[row, row, par, par],
        out_specs=tuple(out_specs),
        compiler_params=_params("arbitrary"),
        name="residual_ln",
    )(x, mix, g.reshape(-1, 1, d), b.reshape(-1, 1, d))


def rotary_tables(seq):
    half = ROPE_DIM // 2
    inv_freq = ROPE_THETA ** (-jnp.arange(half, dtype=F32) / half)
    ang = jnp.arange(seq, dtype=F32)[:, None] * inv_freq[None, :]
    cos, sin = jnp.cos(ang), jnp.sin(ang)
    ones = jnp.ones((seq, HEAD_DIM - ROPE_DIM), F32)
    zeros = jnp.zeros((seq, HEAD_DIM - ROPE_DIM), F32)
    zh = jnp.zeros((seq, half), F32)
    c = jnp.concatenate([cos, cos, ones], axis=-1)
    s_lo = jnp.concatenate([-sin, zh, zeros], axis=-1)
    s_hi = jnp.concatenate([zh, sin, zeros], axis=-1)
    return c, s_lo, s_hi


def _dswa_kernel(q_ref, k_ref, v_ref, c_ref, slo_ref, shi_ref, o_ref, qs, ks, acc, ms, ls, *, seq):
    half = ROPE_DIM // 2
    rows = 256

    def prep(i, carry):
        sl = pl.ds(pl.multiple_of(i * rows, rows), rows)
        c, slo, shi = c_ref[sl, :], slo_ref[sl, :], shi_ref[sl, :]
        for src, dst in ((q_ref, qs), (k_ref, ks)):
            x = src[sl, :]
            dst[sl, :] = (x * c + pltpu.roll(x, HEAD_DIM - half, 1) * slo
                          + pltpu.roll(x, half, 1) * shi)
        acc[sl, :] = jnp.zeros((rows, HEAD_DIM), F32)
        ls[sl, :] = jnp.zeros((rows, HEAD_DIM), F32)
        ms[sl, :] = jnp.full((rows, HEAD_DIM), NEG_INF, F32)
        return carry

    lax.fori_loop(0, seq // rows, prep, 0)

    qb = DSWA_QUERY_BLOCK
    qi = lax.broadcasted_iota(jnp.int32, (qb, BAND + qb), 0)
    kj = lax.broadcasted_iota(jnp.int32, (qb, BAND + qb), 1)
    band_mask = (kj >= qi) & (kj <= qi + BAND)
    qi0 = lax.broadcasted_iota(jnp.int32, (qb, qb), 0)
    kj0 = lax.broadcasted_iota(jnp.int32, (qb, qb), 1)
    first_mask = (kj0 <= qi0) & (kj0 >= qi0 - BAND)

    def scores(q, k2):
        return lax.dot_general(q.astype(BF16), k2.astype(BF16), (((1,), (1,)), ((), ())),
                               preferred_element_type=F32)

    def process(cur, prev):
        if prev is None:
            k2, v2, mask = ks[cur, :], v_ref[cur, :], first_mask
        else:
            k2 = jnp.concatenate([ks[prev, :], ks[cur, :]], axis=0)
            v2 = jnp.concatenate([v_ref[prev, :], v_ref[cur, :]], axis=0)
            mask = band_mask
        s = jnp.where(mask, scores(qs[cur, :], k2), NEG_INF)
        m_old = ms[cur, :]
        m_new = jnp.maximum(m_old[:, :1], jnp.max(s, axis=1, keepdims=True))
        alpha = jnp.exp2(m_old - m_new)
        p = jnp.exp2(s - m_new)
        ls[cur, :] = alpha * ls[cur, :] + jnp.sum(p, axis=1, keepdims=True)
        acc[cur, :] = alpha * acc[cur, :] + jnp.dot(p.astype(BF16), v2.astype(BF16),
                                                    preferred_element_type=F32)
        ms[cur, :] = jnp.broadcast_to(m_new, (qb, HEAD_DIM))

    for d in DSWA_DILATIONS:
        n_blk = seq // (d * qb)
        step = d * qb
        for r in range(d):
            def rows_of(start, size):
                return pl.ds(start, size) if d == 1 else pl.ds(start, size, stride=d)

            def body(n, carry):
                first = r + n * step
                process(rows_of(first, qb), rows_of(first - d * BAND, BAND))
                return carry

            process(rows_of(r, qb), None)
            lax.fori_loop(1, n_blk, body, 0, unroll=2)

    def fin(i, carry):
        sl = pl.ds(pl.multiple_of(i * rows, rows), rows)
        o_ref[sl, :] = (acc[sl, :] / ls[sl, :]).astype(o_ref.dtype)
        return carry

    lax.fori_loop(0, seq // rows, fin, 0)


def dilated_mixture_attention(h, tables, *, q_col, k_col, v_col, n_heads):
    b, seq, _ = h.shape
    assert seq % (max(DSWA_DILATIONS) * DSWA_QUERY_BLOCK) == 0
    col = lambda c0: pl.BlockSpec((None, seq, HEAD_DIM), lambda bi, hi: (bi, 0, c0 + hi))
    tab = pl.BlockSpec((seq, HEAD_DIM), lambda bi, hi: (0, 0))
    vm = lambda: pltpu.VMEM((seq, HEAD_DIM), F32)
    return pl.pallas_call(
        functools.partial(_dswa_kernel, seq=seq),
        out_shape=jax.ShapeDtypeStruct((b, seq, n_heads * HEAD_DIM), BF16),
        grid=(b, n_heads),
        in_specs=[col(q_col), col(k_col), col(v_col), tab, tab, tab],
        out_specs=pl.BlockSpec((None, seq, HEAD_DIM), lambda bi, hi: (bi, 0, hi)),
        scratch_shapes=[vm(), vm(), vm(), vm(), vm()],
        compiler_params=_params("arbitrary", "arbitrary"),
        name="dswa",
    )(h, h, h, *tables)


def _nt(a, b):
    return lax.dot_general(a.astype(BF16), b.astype(BF16), (((1,), (1,)), ((), ())),
                           preferred_element_type=F32)


def _nn(a, b):
    return jnp.dot(a.astype(BF16), b.astype(BF16), preferred_element_type=F32)


def _gdn_kernel(alog_ref, dtb_ref, q_ref, k_ref, v_ref, gate_ref, cq_ref, ck_ref, cv_ref, ab_ref,
                nw_ref, o_ref, qn, kn, vn, gcs, gct, bts, us, wq, ak, *, seq):
    hi = pl.program_id(1)
    c = GDN_CHUNK
    rows = 2 * c
    halo = 8
    neg_a = -jnp.exp(jnp.full((1, 1), alog_ref[hi], F32))
    dt_b = dtb_ref[hi]
    row_i = lax.broadcasted_iota(jnp.int32, (c, c), 0)
    col_i = lax.broadcasted_iota(jnp.int32, (c, c), 1)
    tri_incl = (col_i <= row_i).astype(F32)
    eye = (col_i == row_i).astype(F32)
    lane32 = lax.broadcasted_iota(jnp.int32, (rows, 2 * GDN_HEADS), 1)

    def conv_silu(xh, w_ref):
        y = xh[halo:, :] * w_ref[GDN_CONV - 1:GDN_CONV, :]
        for j in range(1, GDN_CONV):
            y = y + pltpu.roll(xh, j, 0)[halo:, :] * w_ref[GDN_CONV - 1 - j:GDN_CONV - j, :]
        return y * (1.0 / (1.0 + jnp.exp(-y)))

    def l2n(t):
        return t * lax.rsqrt(jnp.sum(t * t, axis=-1, keepdims=True) + NORM_EPS)

    def prep(i, first):
        r0 = pl.multiple_of(i * rows, rows)
        sl = pl.ds(r0, rows)

        def with_halo(ref):
            if first:
                return jnp.concatenate([jnp.zeros((halo, HEAD_DIM), F32), ref[sl, :]], axis=0)
            return ref[pl.ds(pl.multiple_of(r0 - halo, halo), rows + halo), :]

        qn[sl, :] = l2n(conv_silu(with_halo(q_ref), cq_ref)) * (HEAD_DIM ** -0.5)
        kn[sl, :] = l2n(conv_silu(with_halo(k_ref), ck_ref))
        vn[sl, :] = conv_silu(with_halo(v_ref), cv_ref)
        ab = ab_ref[sl, :]
        b_logit = jnp.sum(jnp.where(lane32 == hi, ab, 0.0), axis=1, keepdims=True)
        a_logit = jnp.sum(jnp.where(lane32 == hi + GDN_HEADS, ab, 0.0), axis=1, keepdims=True)
        z = a_logit + dt_b
        g = neg_a * (jnp.maximum(z, 0.0) + jnp.log(1.0 + jnp.exp(-jnp.abs(z))))
        bts[sl, :] = jnp.broadcast_to(1.0 / (1.0 + jnp.exp(-b_logit)), (rows, HEAD_DIM))
        gb = jnp.broadcast_to(g, (rows, HEAD_DIM))
        for half in range(rows // c):
            hs = pl.ds(pl.multiple_of(r0 + half * c, c), c)
            gc = jnp.dot(tri_incl, gb[half * c:(half + 1) * c, :], precision=lax.Precision.HIGHEST,
                         preferred_element_type=F32)
            gcs[hs, :] = gc
            gct[hs, :] = gc.T

    prep(0, True)
    lax.fori_loop(1, seq // rows, lambda i, carry: (prep(i, False), carry)[1], 0)

    hd = HEAD_DIM
    left = lax.broadcasted_iota(jnp.int32, (c, 2 * c), 1) < c
    causal2 = jnp.concatenate([col_i <= row_i, col_i <= row_i], axis=1)
    strict2 = jnp.concatenate([col_i < row_i, col_i < row_i], axis=1)
    eye2 = jnp.concatenate([eye, eye], axis=1)

    def pair_diag(x2):
        return jnp.concatenate([jnp.where(left, x2, 0.0), jnp.where(left, 0.0, x2)], axis=0)

    def independent(qa, ka, va, ba, gca, gra, qb, kb_, vb, bb, gcb, grb):
        lanes = lambda xa, xb: jnp.concatenate([xa, xb], axis=1)
        q2, k2, beta2, gc2 = lanes(qa, qb), lanes(ka, kb_), lanes(ba, bb), lanes(gca, gcb)
        decay2 = jnp.exp(jnp.where(causal2, gc2 - lanes(gra, grb), -jnp.inf))
        kbeta2 = k2 * beta2
        kq = _nt(jnp.concatenate([kbeta2, q2], axis=0), pair_diag(k2))
        nm = kq[:c, :] * jnp.where(strict2, decay2, 0.0)
        attn2 = kq[c:, :] * decay2
        inv = eye2 - nm
        pw = _nn(nm, pair_diag(nm))
        for lvl in range(6):
            if lvl < 5:
                both = _nn(jnp.concatenate([inv, pw], axis=0), pair_diag(pw))
                inv, pw = inv + both[:c, :], both[c:, :]
            else:
                inv = inv + _nn(inv, pair_diag(pw))
        egc2 = jnp.exp(gc2)
        vbeta2, kbe2, qd2 = lanes(va, vb) * beta2, kbeta2 * egc2, q2 * egc2
        rhs_a = jnp.concatenate([vbeta2[:, :hd], kbe2[:, :hd]], axis=1)
        rhs_b = jnp.concatenate([vbeta2[:, hd:], kbe2[:, hd:]], axis=1)
        zero = jnp.zeros_like(rhs_a)
        uw = _nn(inv, jnp.concatenate([jnp.concatenate([rhs_a, zero], axis=1),
                                       jnp.concatenate([zero, rhs_b], axis=1)], axis=0))
        out = []
        for h, (k, gc) in enumerate(((ka, gca), (kb_, gcb))):
            lo = h * 2 * hd
            kdt = (k * jnp.exp(gc[c - 1:c, :] - gc)).T
            out.append((uw[:, lo:lo + hd],
                        jnp.concatenate([uw[:, lo + hd:lo + 2 * hd], qd2[:, h * hd:(h + 1) * hd]],
                                        axis=0).astype(BF16),
                        jnp.concatenate([attn2[:, h * hd:(h + 1) * hd], kdt], axis=0).astype(BF16)))
        return out

    group = GDN_INTERLEAVE
    assert group % 2 == 0

    def independent_group(i, carry):
        sls = [pl.ds(pl.multiple_of((i * group + u) * c, c), c) for u in range(group)]
        s2s = [pl.ds(pl.multiple_of((i * group + u) * (2 * c), 2 * c), 2 * c) for u in range(group)]
        loaded = [tuple(ref[sl, :] for ref in (qn, kn, vn, bts, gcs, gct)) for sl in sls]
        results = []
        for u in range(0, group, 2):
            results += independent(*loaded[u], *loaded[u + 1])
        for sl, s2, (u_new, wq_new, ak_new) in zip(sls, s2s, results):
            us[sl, :] = u_new
            wq[s2, :] = wq_new
            ak[s2, :] = ak_new
        return carry

    lax.fori_loop(0, seq // (c * group), independent_group, 0)

    def scan(n, state):
        sl = pl.ds(pl.multiple_of(n * c, c), c)
        s2 = pl.ds(pl.multiple_of(n * (2 * c), 2 * c), 2 * c)
        ws_qs = _nn(wq[s2, :], state)
        v_new = us[sl, :] - ws_qs[:c, :]
        av_kv = _nn(ak[s2, :], v_new)
        o = ws_qs[c:, :] + av_kv[:c, :]
        last = jnp.exp(gcs[pl.ds(n * c + c - 1, 1), :])
        o = o * lax.rsqrt(jnp.mean(o * o, axis=-1, keepdims=True) + NORM_EPS) * nw_ref[...]
        gate = gate_ref[sl, :]
        o_ref[sl, :] = (o * (gate * (1.0 / (1.0 + jnp.exp(-gate))))).astype(o_ref.dtype)
        return state * last + av_kv[c:, :]

    lax.fori_loop(0, seq // c, scan, jnp.zeros((HEAD_DIM, HEAD_DIM), F32))


def gated_delta_net(h, ab, conv_w, a_log, dt_bias, norm_w, *, q_col, k_col, v_col, gate_col):
    b, seq, _ = h.shape
    nh = GDN_HEADS
    assert seq % (max(2, GDN_INTERLEAVE) * GDN_CHUNK) == 0
    col = lambda c0: pl.BlockSpec((None, seq, HEAD_DIM), lambda bi, hi, *_: (bi, 0, c0 + hi))
    cw = lambda c0: pl.BlockSpec((GDN_CONV, HEAD_DIM), lambda bi, hi, *_: (0, c0 + hi))
    f32s = lambda: pltpu.VMEM((seq, HEAD_DIM), F32)
    b16s = lambda: pltpu.VMEM((2 * seq, HEAD_DIM), BF16)
    grid_spec = pltpu.PrefetchScalarGridSpec(
        num_scalar_prefetch=2,
        grid=(b, nh),
        in_specs=[col(q_col), col(k_col), col(v_col), col(gate_col), cw(0), cw(nh), cw(2 * nh),
                  pl.BlockSpec((None, seq, 2 * nh), lambda bi, hi, *_: (bi, 0, 0)),
                  pl.BlockSpec((1, HEAD_DIM), lambda bi, hi, *_: (0, 0))],
        out_specs=pl.BlockSpec((None, seq, HEAD_DIM), lambda bi, hi, *_: (bi, 0, hi)),
        scratch_shapes=[f32s(), f32s(), f32s(), f32s(), f32s(), f32s(), f32s(), b16s(), b16s()],
    )
    return pl.pallas_call(
        functools.partial(_gdn_kernel, seq=seq),
        out_shape=jax.ShapeDtypeStruct((b, seq, nh * HEAD_DIM), BF16),
        grid_spec=grid_spec,
        compiler_params=_params("arbitrary", "arbitrary"),
        name="gdn",
    )(a_log, dt_bias, h, h, h, h, conv_w, conv_w, conv_w, ab, norm_w.reshape(1, HEAD_DIM))


def _fox_cum_kernel(fl_ref, bf_ref, o_ref, *, seq):
    c = 128
    tri_incl = (lax.broadcasted_iota(jnp.int32, (c, c), 1)
                <= lax.broadcasted_iota(jnp.int32, (c, c), 0)).astype(F32)

    def body(n, carry):
        sl = pl.ds(pl.multiple_of(n * c, c), c)
        z = fl_ref[sl, :] + bf_ref[...]
        log_f = -(jnp.maximum(-z, 0.0) + jnp.log(1.0 + jnp.exp(-jnp.abs(z))))
        cs = jnp.dot(tri_incl, log_f, precision=lax.Precision.HIGHEST,
                     preferred_element_type=F32) + carry
        o_ref[sl, :] = cs * LOG2E
        return cs[c - 1:c, :]

    lax.fori_loop(0, seq // c, body, jnp.zeros((1, fl_ref.shape[-1]), F32))


def forget_gate_cumsum(f_logit, b_f):
    b, seq, nh = f_logit.shape
    blk = pl.BlockSpec((None, seq, nh), lambda bi: (bi, 0, 0))
    return pl.pallas_call(
        functools.partial(_fox_cum_kernel, seq=seq),
        out_shape=jax.ShapeDtypeStruct((b, seq, nh), F32),
        grid=(b,),
        in_specs=[blk, pl.BlockSpec((1, nh), lambda bi: (0, 0))],
        out_specs=blk,
        compiler_params=_params("arbitrary"),
        name="fox_cumsum",
    )(f_logit, b_f.reshape(1, nh))


def _fox_kernel(q_ref, k_ref, v_ref, cum_ref, o_ref, *, blk, heads):
    qi = pl.program_id(2)
    hd = HEAD_DIM
    ri = lax.broadcasted_iota(jnp.int32, (blk, blk), 0)
    ci = lax.broadcasted_iota(jnp.int32, (blk, blk), 1)
    eye = (lax.broadcasted_iota(jnp.int32, (hd, hd), 0)
           == lax.broadcasted_iota(jnp.int32, (hd, hd), 1))
    qs = pl.ds(pl.multiple_of(qi * blk, blk), blk)

    def as_column(row):
        return jnp.concatenate(
            [jnp.sum(jnp.where(eye, row[:, c * hd:(c + 1) * hd], 0.0), axis=1, keepdims=True)
             for c in range(blk // hd)], axis=0)

    cq = [as_column(cum_ref[g:g + 1, qs]) for g in range(heads)]
    q = [q_ref[:, g * hd:(g + 1) * hd] for g in range(heads)]

    def step(j, carry, diagonal):
        ks = pl.ds(pl.multiple_of(j * blk, blk), blk)
        out = []
        for g in range(heads):
            m, l, acc = carry[g]
            s = _nt(q[g], k_ref[ks, g * hd:(g + 1) * hd]) + (cq[g] - cum_ref[g:g + 1, ks])
            if diagonal:
                s = jnp.where(ri >= ci, s, NEG_INF)
            m_new = jnp.maximum(m, jnp.max(s, axis=1, keepdims=True))
            alpha = jnp.exp2(m - m_new)
            p = jnp.exp2(s - m_new)
            l = alpha * l + jnp.sum(p, axis=1, keepdims=True)
            acc = alpha * acc + jnp.dot(p.astype(BF16), v_ref[ks, g * hd:(g + 1) * hd],
                                        preferred_element_type=F32)
            out.append((m_new, l, acc))
        return tuple(out)

    init = tuple((jnp.full((blk, 1), NEG_INF, F32), jnp.zeros((blk, 1), F32),
                  jnp.zeros((blk, hd), F32)) for _ in range(heads))
    carry = lax.fori_loop(0, qi, lambda j, cr: step(j, cr, False), init)
    carry = step(qi, carry, True)
    for g in range(heads):
        _, l, acc = carry[g]
        o_ref[:, g * hd:(g + 1) * hd] = (acc / l).astype(o_ref.dtype)


def forgetting_attention(hq, cum_rows, *, n_heads, blk=512, heads=2):
    b, seq, _ = hq.shape
    assert seq % blk == 0 and n_heads % heads == 0
    w = heads * HEAD_DIM
    ng = n_heads // heads
    kv = lambda c0: pl.BlockSpec((None, seq, w), lambda bi, hi, qi: (bi, 0, c0 + hi))
    return pl.pallas_call(
        functools.partial(_fox_kernel, blk=blk, heads=heads),
        out_shape=jax.ShapeDtypeStruct((b, seq, n_heads * HEAD_DIM), BF16),
        grid=(b, ng, seq // blk),
        in_specs=[pl.BlockSpec((None, blk, w), lambda bi, hi, qi: (bi, qi, hi)),
                  kv(ng), kv(2 * ng),
                  pl.BlockSpec((None, None, heads, seq), lambda bi, hi, qi: (bi, hi, 0, 0))],
        out_specs=pl.BlockSpec((None, blk, w), lambda bi, hi, qi: (bi, qi, hi)),
        compiler_params=_params("arbitrary", "arbitrary", "arbitrary"),
        name="fox",
    )(hq, hq, hq, cum_rows)


def _router_kernel(x_ref, rw_ref, rb_ref, idx_ref, gate_ref):
    logits = lax.dot_general(rw_ref[...], x_ref[...], (((1,), (1,)), ((), ())),
                             precision=lax.Precision.HIGHEST, preferred_element_type=F32)
    score = 1.0 / (1.0 + jnp.exp(-logits))
    sel = score + rb_ref[...]
    sel_r = [sel[e:e + 1, :] for e in range(N_EXPERTS)]
    score_r = [score[e:e + 1, :] for e in range(N_EXPERTS)]

    def top2_sum(a, b, c, d):
        hi1, lo1, hi2, lo2 = jnp.maximum(a, b), jnp.minimum(a, b), jnp.maximum(c, d), jnp.minimum(c, d)
        return jnp.maximum(hi1, hi2) + jnp.maximum(jnp.minimum(hi1, hi2), jnp.maximum(lo1, lo2))

    epg = EXPERTS_PER_GROUP
    best = top2_sum(*sel_r[:epg])
    grp = jnp.zeros_like(best, dtype=jnp.int32)
    for g in range(1, N_GROUPS):
        cand = top2_sum(*sel_r[g * epg:(g + 1) * epg])
        take = cand > best
        best = jnp.where(take, cand, best)
        grp = jnp.where(take, g, grp)
    in_sel, in_score = [], []
    for j in range(epg):
        s_j, u_j = sel_r[j], score_r[j]
        for g in range(1, N_GROUPS):
            s_j = jnp.where(grp == g, sel_r[g * epg + j], s_j)
            u_j = jnp.where(grp == g, score_r[g * epg + j], u_j)
        in_sel.append(s_j)
        in_score.append(u_j)

    def arg_best(vals):
        bv, bi = vals[0], jnp.zeros_like(grp)
        for j in range(1, epg):
            take = vals[j] > bv
            bv = jnp.where(take, vals[j], bv)
            bi = jnp.where(take, j, bi)
        return bi

    i1 = arg_best(in_sel)
    i2 = arg_best([jnp.where(i1 == j, -jnp.inf, in_sel[j]) for j in range(epg)])

    def pick(vals, i):
        out = vals[0]
        for j in range(1, epg):
            out = jnp.where(i == j, vals[j], out)
        return out

    g1, g2 = pick(in_score, i1), pick(in_score, i2)
    tot = g1 + g2
    idx_ref[0:1, :] = grp * epg + i1
    idx_ref[1:2, :] = grp * epg + i2
    gate_ref[0:1, :] = g1 / tot
    gate_ref[1:2, :] = g2 / tot


def route(x, router_w, router_bias, *, tm=512):
    t, d = x.shape
    tm = min(tm, t)
    out = pl.BlockSpec((2, tm), lambda i: (0, i))
    return pl.pallas_call(
        _router_kernel,
        out_shape=(jax.ShapeDtypeStruct((2, t), jnp.int32), jax.ShapeDtypeStruct((2, t), F32)),
        grid=(t // tm,),
        in_specs=[pl.BlockSpec((tm, d), lambda i: (i, 0)),
                  pl.BlockSpec((N_EXPERTS, d), lambda i: (0, 0)),
                  pl.BlockSpec((N_EXPERTS, 1), lambda i: (0, 0))],
        out_specs=(out, out),
        compiler_params=_params("arbitrary"),
        name="router",
    )(x, router_w.T, router_bias.reshape(N_EXPERTS, 1))


def _moe_up_kernel(be_ref, first_ref, nused_ref, x_ref, wg_ref, wu_ref, o_ref, wgb, wub):
    i = pl.program_id(1)

    @pl.when(i < nused_ref[0])
    def _():
        @pl.when(first_ref[i] == 1)
        def _():
            wgb[...] = wg_ref[...].astype(BF16)
            wub[...] = wu_ref[...].astype(BF16)

        xs = x_ref[...]
        a = jnp.dot(xs, wgb[...], preferred_element_type=F32)
        u = jnp.dot(xs, wub[...], preferred_element_type=F32)
        o_ref[...] = (a * (1.0 / (1.0 + jnp.exp(-a))) * u).astype(o_ref.dtype)

    @pl.when(i >= nused_ref[0])
    def _():
        o_ref[...] = jnp.zeros_like(o_ref)


def _moe_down_kernel(be_ref, nused_ref, h_ref, wd_ref, o_ref, *, rows, tn, d_chunks):
    i, j = pl.program_id(0), pl.program_id(1)
    n_chunks = tn // V7X_LANES
    used = i < nused_ref[0]
    y = jnp.where(used, jnp.dot(h_ref[...], wd_ref[...].astype(BF16), preferred_element_type=F32), 0.0)
    for cc in range(n_chunks):
        o_ref[pl.ds(j * n_chunks + cc, rows, stride=ROW_STRIDE), :] = (
            y[:, cc * V7X_LANES:(cc + 1) * V7X_LANES])

    @pl.when(j == 0)
    def _():
        for c in range(d_chunks, ROW_STRIDE):
            o_ref[pl.ds(c, rows, stride=ROW_STRIDE), :] = jnp.zeros((rows, V7X_LANES), F32)


def expert_ffn(xg, block_expert, block_first, n_used, w_gate, w_up, w_down, layer, *, tn_up=256,
               tn_down=1024):
    r, d = xg.shape
    de = w_gate.shape[-1]
    nb = r // MOE_ROWS
    w_in = pl.BlockSpec((None, None, d, tn_up), lambda j, i, be, fi, nu: (layer, be[i], 0, j))
    hid = pl.pallas_call(
        _moe_up_kernel,
        out_shape=jax.ShapeDtypeStruct((r, de), BF16),
        grid_spec=pltpu.PrefetchScalarGridSpec(
            num_scalar_prefetch=3,
            grid=(de // tn_up, nb),
            in_specs=[pl.BlockSpec((MOE_ROWS, d), lambda j, i, *_: (i, 0)), w_in, w_in],
            out_specs=pl.BlockSpec((MOE_ROWS, tn_up), lambda j, i, *_: (i, j)),
            scratch_shapes=[pltpu.VMEM((d, tn_up), BF16), pltpu.VMEM((d, tn_up), BF16)],
        ),
        compiler_params=_params("arbitrary", "arbitrary"),
        name="moe_up",
    )(block_expert, block_first, n_used, xg, w_gate, w_up)
    d_chunks = d // V7X_LANES
    assert d_chunks <= ROW_STRIDE
    return pl.pallas_call(
        functools.partial(_moe_down_kernel, rows=MOE_ROWS, tn=tn_down, d_chunks=d_chunks),
        out_shape=jax.ShapeDtypeStruct((r * ROW_STRIDE, V7X_LANES), F32),
        grid_spec=pltpu.PrefetchScalarGridSpec(
            num_scalar_prefetch=2,
            grid=(nb, d // tn_down),
            in_specs=[pl.BlockSpec((MOE_ROWS, de), lambda i, j, *_: (i, 0)),
                      pl.BlockSpec((None, None, de, tn_down),
                                   lambda i, j, be, nu: (layer, be[i], 0, j))],
            out_specs=pl.BlockSpec((MOE_ROWS * ROW_STRIDE, V7X_LANES), lambda i, j, *_: (i, 0)),
        ),
        compiler_params=_params("arbitrary", "arbitrary"),
        name="moe_down",
    )(block_expert, n_used, hid, w_down)


DMA_PRIORITIES = 2
DMA_BURST = 8


def _token_row_copy(src_ref, src_tok, dst_ref, dst_tok, n_chunks, sem):
    src = pl.ds(pl.multiple_of(src_tok * ROW_STRIDE, ROW_STRIDE), n_chunks)
    dst = pl.ds(pl.multiple_of(dst_tok * ROW_STRIDE, ROW_STRIDE), n_chunks)
    return pltpu.make_async_copy(src_ref.at[src], dst_ref.at[dst], sem)


def _gather_tokens(idx_of, src_ref, buf_ref, n, n_chunks, sem):
    assert n % DMA_BURST == 0

    def issue(g, carry):
        for u in range(DMA_BURST):
            r = g * DMA_BURST + u
            _token_row_copy(src_ref, idx_of(r), buf_ref, r, n_chunks, sem).start(
                priority=u % DMA_PRIORITIES)
        return carry

    def drain(g, carry):
        for u in range(DMA_BURST):
            _token_row_copy(src_ref, 0, buf_ref, g * DMA_BURST + u, n_chunks, sem).wait()
        return carry

    lax.fori_loop(0, n // DMA_BURST, issue, 0)
    lax.fori_loop(0, n // DMA_BURST, drain, 0)


def _gather_kernel(idx_ref, nused_ref, src_ref, o_ref, buf, sem, *, rows):
    i = pl.program_id(0)
    n_chunks = o_ref.shape[1] // V7X_LANES

    @pl.when(i < nused_ref[0])
    def _():
        _gather_tokens(lambda r: idx_ref[i * rows + r], src_ref, buf, rows, n_chunks, sem)
        for c in range(n_chunks):
            o_ref[:, c * V7X_LANES:(c + 1) * V7X_LANES] = _token_chunk(buf, c, rows).astype(BF16)

    @pl.when(i >= nused_ref[0])
    def _():
        o_ref[...] = jnp.zeros_like(o_ref)


def gather_rows(src_rows, idx, n_used, d, *, rows=MOE_ROWS):
    m = idx.shape[0]
    assert m % rows == 0
    return pl.pallas_call(
        functools.partial(_gather_kernel, rows=rows),
        out_shape=jax.ShapeDtypeStruct((m, d), BF16),
        grid_spec=pltpu.PrefetchScalarGridSpec(
            num_scalar_prefetch=2,
            grid=(m // rows,),
            in_specs=[pl.BlockSpec(memory_space=pl.ANY)],
            out_specs=pl.BlockSpec((rows, d), lambda i, *_: (i, 0)),
            scratch_shapes=[pltpu.VMEM((rows * ROW_STRIDE, V7X_LANES), F32),
                            pltpu.SemaphoreType.DMA(())],
        ),
        compiler_params=_params("arbitrary"),
        name="gather_rows",
    )(idx, n_used, src_rows)


def _combine_ln_kernel(pos_ref, x_ref, y_ref, gate_ref, g_ref, b_ref, o_ref, ob_ref, ya, yb, sem,
                       *, tm, n_tok):
    i = pl.program_id(0)
    n_chunks = x_ref.shape[1] // V7X_LANES
    _gather_tokens(lambda r: pos_ref[i * tm + r], y_ref, ya, tm, n_chunks, sem)
    _gather_tokens(lambda r: pos_ref[n_tok + i * tm + r], y_ref, yb, tm, n_chunks, sem)
    g0, g1 = gate_ref[:, 0:1], gate_ref[:, 1:2]
    ffn = jnp.concatenate([_token_chunk(ya, c, tm) * g0 + _token_chunk(yb, c, tm) * g1
                           for c in range(n_chunks)], axis=1)
    out = _layer_norm_rows(DEEPNORM_ALPHA * x_ref[...] + ffn, g_ref, b_ref)
    o_ref[...] = out
    ob_ref[...] = out.astype(BF16)


def moe_combine_layer_norm(x, y_rows, pos, gate, g, b, layer, *, tm=256):
    t, d = x.shape
    tm = min(tm, t)
    row = pl.BlockSpec((tm, d), lambda i, *_: (i, 0))
    par = pl.BlockSpec((None, 1, d), lambda i, *_: (layer, 0, 0))
    buf = lambda: pltpu.VMEM((tm * ROW_STRIDE, V7X_LANES), F32)
    return pl.pallas_call(
        functools.partial(_combine_ln_kernel, tm=tm, n_tok=t),
        out_shape=(jax.ShapeDtypeStruct((t, d), F32), jax.ShapeDtypeStruct((t, d), BF16)),
        grid_spec=pltpu.PrefetchScalarGridSpec(
            num_scalar_prefetch=1,
            grid=(t // tm,),
            in_specs=[row, pl.BlockSpec(memory_space=pl.ANY),
                      pl.BlockSpec((tm, 2), lambda i, *_: (i, 0)), par, par],
            out_specs=(row, row),
            scratch_shapes=[buf(), buf(), pltpu.SemaphoreType.DMA(())],
        ),
        compiler_params=_params("arbitrary"),
        name="moe_combine_ln",
    )(pos, x, y_rows, gate, g.reshape(-1, 1, d), b.reshape(-1, 1, d))


def moe_ffn(x, x_rows, router_w, router_bias, w_gate, w_up, w_down, layer):
    t, d = x.shape
    idx, gate = route(x, router_w, router_bias)
    flat_e = idx.reshape(-1)
    n_assign = flat_e.shape[0]
    order = jnp.argsort(flat_e).astype(jnp.int32)
    e_sorted = flat_e[order]
    bounds = jnp.searchsorted(e_sorted, jnp.arange(N_EXPERTS + 1, dtype=jnp.int32)).astype(jnp.int32)
    start, counts = bounds[:-1], bounds[1:] - bounds[:-1]
    padded = (counts + MOE_ROWS - 1) // MOE_ROWS * MOE_ROWS
    pad_end = jnp.cumsum(padded)
    pad_start = pad_end - padded
    dest_sorted = pad_start[e_sorted] + jnp.arange(n_assign, dtype=jnp.int32) - start[e_sorted]
    n_blocks = -(-(n_assign + N_EXPERTS * (MOE_ROWS - 1)) // MOE_ROWS)
    rows = n_blocks * MOE_ROWS
    blk_start = jnp.arange(n_blocks, dtype=jnp.int32) * MOE_ROWS
    block_expert = jnp.minimum(jnp.searchsorted(pad_end, blk_start, side='right'),
                               N_EXPERTS - 1).astype(jnp.int32)
    n_used = (pad_end[-1] // MOE_ROWS).astype(jnp.int32).reshape(1)
    block_first = jnp.concatenate([jnp.ones((1,), jnp.int32),
                                   (block_expert[1:] != block_expert[:-1]).astype(jnp.int32)])
    tok_sorted = order % t
    row_tok = (jnp.arange(rows, dtype=jnp.int32) % t).at[dest_sorted].set(tok_sorted)
    pos = jnp.zeros((n_assign,), jnp.int32).at[order].set(dest_sorted)
    xg = gather_rows(x_rows, row_tok, n_used, d)
    y_rows = expert_ffn(xg, block_expert, block_first, n_used, w_gate, w_up, w_down, layer)
    return y_rows, pos, gate.T


def _q_scale(n_q, n_total):
    return jnp.concatenate([jnp.full((1, n_q), ATTN_Q_SCALE, F32),
                            jnp.ones((1, n_total - n_q), F32)], axis=1)


def kernel(x, w_in_even, conv_w, a_log, dt_bias, gdn_norm_w, w_out_even, w_in_odd, b_f, w_out_odd, ln_mix_g, ln_mix_b, ln_ffn_g, ln_ffn_b, w_gate, w_up, w_down, router_w, router_bias):
    b, seq, d = x.shape
    t = b * seq
    xf = x.reshape(t, d)
    xb = xf.astype(BF16)
    tables = rotary_tables(seq)
    hb = HEAD_DIM
    for layer in range(DEPTH):
        i = layer // 2
        if layer % 2 == 0:
            wa, wqkv, wg = DSWA_HEADS * hb, 3 * GDN_HEADS * hb, GDN_HEADS * hb
            n_main = 3 * wa + wqkv + wg
            w_in = w_in_even[i].astype(BF16)
            h = matmul(xb, w_in, ncols=n_main, col_scale=_q_scale(wa, n_main)).reshape(b, seq, n_main)
            ab = matmul(xb, w_in[:, n_main:]).reshape(b, seq, 2 * GDN_HEADS)
            o_a = dilated_mixture_attention(h, tables, q_col=0, k_col=DSWA_HEADS,
                                            v_col=2 * DSWA_HEADS, n_heads=DSWA_HEADS)
            c0 = 3 * DSWA_HEADS
            o_b = gated_delta_net(h, ab, conv_w[i], a_log[i], dt_bias[i], gdn_norm_w[i],
                                  q_col=c0, k_col=c0 + GDN_HEADS, v_col=c0 + 2 * GDN_HEADS,
                                  gate_col=c0 + 3 * GDN_HEADS)
            o = jnp.concatenate([o_a, o_b], axis=-1).reshape(t, -1)
            mix = matmul(o, w_out_even, lead=i)
        else:
            n_main = 3 * FOX_HEADS * hb
            w_in = w_in_odd[i].astype(BF16)
            hq = matmul(xb, w_in, ncols=n_main, col_scale=_q_scale(FOX_HEADS * hb, n_main),
                        out_dtype=BF16).reshape(b, seq, n_main)
            f_logit = matmul(xb, w_in[:, n_main:]).reshape(b, seq, FOX_HEADS)
            cum = forget_gate_cumsum(f_logit, b_f[i])
            pair = 2
            cum_rows = cum.transpose(0, 2, 1).reshape(b, FOX_HEADS // pair, pair, seq)
            o = forgetting_attention(hq, cum_rows, n_heads=FOX_HEADS, heads=pair).reshape(t, -1)
            mix = matmul(o, w_out_odd, lead=i)
        xf, _, x_rows = residual_layer_norm(xf, mix, ln_mix_g, ln_mix_b, layer, token_rows=True)
        y_rows, pos, gate = moe_ffn(xf, x_rows, router_w, router_bias, w_gate, w_up, w_down, layer)
        xf, xb = moe_combine_layer_norm(xf, y_rows, pos, gate, ln_ffn_g, ln_ffn_b, layer)
    return xf.reshape(b, seq, d)
```

```python
import functools

import jax
import jax.numpy as jnp
import numpy as np
from jax import lax
from jax.experimental import pallas as pl
from jax.experimental.pallas import tpu as pltpu

F32 = jnp.float32
BF16 = jnp.bfloat16

V7X_LANES = 128
V7X_VMEM_BYTES = 64 * 1024 * 1024
VMEM_LIMIT = 56 * 1024 * 1024

HEAD_DIM = 128
DSWA_HEADS = 16
DSWA_DILATIONS = (1, 4, 16)
BAND = 128
DSWA_QUERY_BLOCK = 256
GDN_HEADS = 16
GDN_CONV = 4
GDN_CHUNK = 128
GDN_INTERLEAVE = 4
FOX_HEADS = 32
ROPE_THETA = 500000.0
ROPE_DIM = HEAD_DIM // 4
N_EXPERTS = 32
N_GROUPS = 8
EXPERTS_PER_GROUP = 4
D_EXPERT = 768
MOE_ROWS = 512
IN_PROJ_TN = 1024
LN_EPS = 1e-5
NORM_EPS = 1e-6
NEG_INF = -1e30
DEPTH = 2
DEEPNORM_ALPHA = (2 * DEPTH) ** 0.25
LOG2E = 1.4426950408889634
ATTN_Q_SCALE = HEAD_DIM ** -0.5 * LOG2E


def _params(*sem):
    return pltpu.CompilerParams(dimension_semantics=sem, vmem_limit_bytes=VMEM_LIMIT)


def _mm_kernel(x_ref, w_ref, s_ref, o_ref, *scratch):
    if scratch:
        wb_ref, = scratch

        @pl.when(pl.program_id(1) == 0)
        def _():
            wb_ref[...] = w_ref[...].astype(BF16)
    else:
        wb_ref = w_ref
    acc = jnp.dot(x_ref[...], wb_ref[...], preferred_element_type=F32)
    o_ref[...] = (acc * s_ref[...]).astype(o_ref.dtype)


def matmul(x, w, *, lead=None, col0=0, ncols=None, col_scale=None, out_dtype=F32, tm=1024, tn=512):
    m, k = x.shape
    n_total = w.shape[-1]
    ncols = n_total - col0 if ncols is None else ncols
    tn = min(tn, ncols)
    tm = min(tm, m)
    assert ncols % tn == 0 and col0 % tn == 0 and m % tm == 0
    cb = col0 // tn
    if lead is None:
        w_spec = pl.BlockSpec((k, tn), lambda j, i: (0, j + cb))
    else:
        w_spec = pl.BlockSpec((None, k, tn), lambda j, i: (lead, 0, j + cb))
    if col_scale is None:
        col_scale = jnp.ones((1, ncols), F32)
    return pl.pallas_call(
        _mm_kernel,
        out_shape=jax.ShapeDtypeStruct((m, ncols), out_dtype),
        grid=(ncols // tn, m // tm),
        in_specs=[pl.BlockSpec((tm, k), lambda j, i: (i, 0)), w_spec,
                  pl.BlockSpec((1, tn), lambda j, i: (0, j))],
        out_specs=pl.BlockSpec((tm, tn), lambda j, i: (i, j)),
        scratch_shapes=[] if w.dtype == BF16 else [pltpu.VMEM((k, tn), BF16)],
        compiler_params=_params("arbitrary", "arbitrary"),
        name="matmul",
    )(x, w, col_scale)


ROW_STRIDE = 40


def _store_token_rows(dst_ref, val):
    n, d = val.shape
    for c in range(ROW_STRIDE):
        chunk = val[:, c * V7X_LANES:(c + 1) * V7X_LANES] if c < d // V7X_LANES else jnp.zeros(
            (n, V7X_LANES), F32)
        dst_ref[pl.ds(c, n, stride=ROW_STRIDE), :] = chunk


def _token_chunk(src_ref, c, n):
    return src_ref[pl.ds(c, n, stride=ROW_STRIDE), :]


def _layer_norm_rows(y, g_ref, b_ref):
    mu = jnp.mean(y, axis=-1, keepdims=True)
    yc = y - mu
    var = jnp.mean(yc * yc, axis=-1, keepdims=True)
    return yc * lax.rsqrt(var + LN_EPS) * g_ref[...] + b_ref[...]


def _ln_kernel(x_ref, mix_ref, g_ref, b_ref, o_ref, ob_ref, *maybe_rows_ref):
    out = _layer_norm_rows(DEEPNORM_ALPHA * x_ref[...] + mix_ref[...], g_ref, b_ref)
    o_ref[...] = out
    ob_ref[...] = out.astype(BF16)
    for rows_ref in maybe_rows_ref:
        _store_token_rows(rows_ref, out)


def residual_layer_norm(x, mix, g, b, layer, *, token_rows=False, tm=256):
    m, d = x.shape
    tm = min(tm, m)
    row = pl.BlockSpec((tm, d), lambda i: (i, 0))
    par = pl.BlockSpec((None, 1, d), lambda i: (layer, 0, 0))
    out_shape = [jax.ShapeDtypeStruct((m, d), F32), jax.ShapeDtypeStruct((m, d), BF16)]
    out_specs = [row, row]
    if token_rows:
        out_shape.append(jax.ShapeDtypeStruct((m * ROW_STRIDE, V7X_LANES), F32))
        out_specs.append(pl.BlockSpec((tm * ROW_STRIDE, V7X_LANES), lambda i: (i, 0)))
    return pl.pallas_call(
        _ln_kernel,
        out_shape=tuple(out_shape),
        grid=(m // tm,),
        in_specs=[row, row, par, par],
        out_specs=tuple(out_specs),
        compiler_params=_params("arbitrary"),
        name="residual_ln",
    )(x, mix, g.reshape(-1, 1, d), b.reshape(-1, 1, d))


def rotary_tables(seq):
    half = ROPE_DIM // 2
    inv_freq = ROPE_THETA ** (-jnp.arange(half, dtype=F32) / half)
    ang = jnp.arange(seq, dtype=F32)[:, None] * inv_freq[None, :]
    cos, sin = jnp.cos(ang), jnp.sin(ang)
    ones = jnp.ones((seq, HEAD_DIM - ROPE_DIM), F32)
    zeros = jnp.zeros((seq, HEAD_DIM - ROPE_DIM), F32)
    zh = jnp.zeros((seq, half), F32)
    c = jnp.concatenate([cos, cos, ones], axis=-1)
    s_lo = jnp.concatenate([-sin, zh, zeros], axis=-1)
    s_hi = jnp.concatenate([zh, sin, zeros], axis=-1)
    return c, s_lo, s_hi


def _dswa_kernel(q_ref, k_ref, v_ref, c_ref, slo_ref, shi_ref, o_ref, qs, ks, acc, ms, ls, *, seq):
    half = ROPE_DIM // 2
    rows = 256

    def prep(i, carry):
        sl = pl.ds(pl.multiple_of(i * rows, rows), rows)
        c, slo, shi = c_ref[sl, :], slo_ref[sl, :], shi_ref[sl, :]
        for src, dst in ((q_ref, qs), (k_ref, ks)):
            x = src[sl, :]
            dst[sl, :] = (x * c + pltpu.roll(x, HEAD_DIM - half, 1) * slo
                          + pltpu.roll(x, half, 1) * shi)
        acc[sl, :] = jnp.zeros((rows, HEAD_DIM), F32)
        ls[sl, :] = jnp.zeros((rows, HEAD_DIM), F32)
        ms[sl, :] = jnp.full((rows, HEAD_DIM), NEG_INF, F32)
        return carry

    lax.fori_loop(0, seq // rows, prep, 0)

    qb = DSWA_QUERY_BLOCK
    qi = lax.broadcasted_iota(jnp.int32, (qb, BAND + qb), 0)
    kj = lax.broadcasted_iota(jnp.int32, (qb, BAND + qb), 1)
    band_mask = (kj >= qi) & (kj <= qi + BAND)
    qi0 = lax.broadcasted_iota(jnp.int32, (qb, qb), 0)
    kj0 = lax.broadcasted_iota(jnp.int32, (qb, qb), 1)
    first_mask = (kj0 <= qi0) & (kj0 >= qi0 - BAND)

    def scores(q, k2):
        return lax.dot_general(q.astype(BF16), k2.astype(BF16), (((1,), (1,)), ((), ())),
                               preferred_element_type=F32)

    def process(cur, prev):
        if prev is None:
            k2, v2, mask = ks[cur, :], v_ref[cur, :], first_mask
        else:
            k2 = jnp.concatenate([ks[prev, :], ks[cur, :]], axis=0)
            v2 = jnp.concatenate([v_ref[prev, :], v_ref[cur, :]], axis=0)
            mask = band_mask
        s = jnp.where(mask, scores(qs[cur, :], k2), NEG_INF)
        m_old = ms[cur, :]
        m_new = jnp.maximum(m_old[:, :1], jnp.max(s, axis=1, keepdims=True))
        alpha = jnp.exp2(m_old - m_new)
        p = jnp.exp2(s - m_new)
        ls[cur, :] = alpha * ls[cur, :] + jnp.sum(p, axis=1, keepdims=True)
        acc[cur, :] = alpha * acc[cur, :] + jnp.dot(p.astype(BF16), v2.astype(BF16),
                                                    preferred_element_type=F32)
        ms[cur, :] = jnp.broadcast_to(m_new, (qb, HEAD_DIM))

    for d in DSWA_DILATIONS:
        n_blk = seq // (d * qb)
        step = d * qb
        for r in range(d):
            def rows_of(start, size):
                return pl.ds(start, size) if d == 1 else pl.ds(start, size, stride=d)

            def body(n, carry):
                first = r + n * step
                process(rows_of(first, qb), rows_of(first - d * BAND, BAND))
                return carry

            process(rows_of(r, qb), None)
            lax.fori_loop(1, n_blk, body, 0, unroll=2)

    def fin(i, carry):
        sl = pl.ds(pl.multiple_of(i * rows, rows), rows)
        o_ref[sl, :] = (acc[sl, :] / ls[sl, :]).astype(o_ref.dtype)
        return carry

    lax.fori_loop(0, seq // rows, fin, 0)


def dilated_mixture_attention(h, tables, *, q_col, k_col, v_col, n_heads):
    b, seq, _ = h.shape
    assert seq % (max(DSWA_DILATIONS) * DSWA_QUERY_BLOCK) == 0
    col = lambda c0: pl.BlockSpec((None, seq, HEAD_DIM), lambda bi, hi: (bi, 0, c0 + hi))
    tab = pl.BlockSpec((seq, HEAD_DIM), lambda bi, hi: (0, 0))
    vm = lambda: pltpu.VMEM((seq, HEAD_DIM), F32)
    return pl.pallas_call(
        functools.partial(_dswa_kernel, seq=seq),
        out_shape=jax.ShapeDtypeStruct((b, seq, n_heads * HEAD_DIM), BF16),
        grid=(b, n_heads),
        in_specs=[col(q_col), col(k_col), col(v_col), tab, tab, tab],
        out_specs=pl.BlockSpec((None, seq, HEAD_DIM), lambda bi, hi: (bi, 0, hi)),
        scratch_shapes=[vm(), vm(), vm(), vm(), vm()],
        compiler_params=_params("arbitrary", "arbitrary"),
        name="dswa",
    )(h, h, h, *tables)


def _nt(a, b):
    return lax.dot_general(a.astype(BF16), b.astype(BF16), (((1,), (1,)), ((), ())),
                           preferred_element_type=F32)


def _nn(a, b):
    return jnp.dot(a.astype(BF16), b.astype(BF16), preferred_element_type=F32)


def _gdn_kernel(alog_ref, dtb_ref, q_ref, k_ref, v_ref, gate_ref, cq_ref, ck_ref, cv_ref, ab_ref,
                nw_ref, o_ref, qn, kn, vn, gcs, gct, bts, us, wq, ak, *, seq):
    hi = pl.program_id(1)
    c = GDN_CHUNK
    rows = 2 * c
    halo = 8
    neg_a = -jnp.exp(jnp.full((1, 1), alog_ref[hi], F32))
    dt_b = dtb_ref[hi]
    row_i = lax.broadcasted_iota(jnp.int32, (c, c), 0)
    col_i = lax.broadcasted_iota(jnp.int32, (c, c), 1)
    tri_incl = (col_i <= row_i).astype(F32)
    eye = (col_i == row_i).astype(F32)
    lane32 = lax.broadcasted_iota(jnp.int32, (rows, 2 * GDN_HEADS), 1)

    def conv_silu(xh, w_ref):
        y = xh[halo:, :] * w_ref[GDN_CONV - 1:GDN_CONV, :]
        for j in range(1, GDN_CONV):
            y = y + pltpu.roll(xh, j, 0)[halo:, :] * w_ref[GDN_CONV - 1 - j:GDN_CONV - j, :]
        return y * (1.0 / (1.0 + jnp.exp(-y)))

    def l2n(t):
        return t * lax.rsqrt(jnp.sum(t * t, axis=-1, keepdims=True) + NORM_EPS)

    def prep(i, first):
        r0 = pl.multiple_of(i * rows, rows)
        sl = pl.ds(r0, rows)

        def with_halo(ref):
            if first:
                return jnp.concatenate([jnp.zeros((halo, HEAD_DIM), F32), ref[sl, :]], axis=0)
            return ref[pl.ds(pl.multiple_of(r0 - halo, halo), rows + halo), :]

        qn[sl, :] = l2n(conv_silu(with_halo(q_ref), cq_ref)) * (HEAD_DIM ** -0.5)
        kn[sl, :] = l2n(conv_silu(with_halo(k_ref), ck_ref))
        vn[sl, :] = conv_silu(with_halo(v_ref), cv_ref)
        ab = ab_ref[sl, :]
        b_logit = jnp.sum(jnp.where(lane32 == hi, ab, 0.0), axis=1, keepdims=True)
        a_logit = jnp.sum(jnp.where(lane32 == hi + GDN_HEADS, ab, 0.0), axis=1, keepdims=True)
        z = a_logit + dt_b
        g = neg_a * (jnp.maximum(z, 0.0) + jnp.log(1.0 + jnp.exp(-jnp.abs(z))))
        bts[sl, :] = jnp.broadcast_to(1.0 / (1.0 + jnp.exp(-b_logit)), (rows, HEAD_DIM))
        gb = jnp.broadcast_to(g, (rows, HEAD_DIM))
        for half in range(rows // c):
            hs = pl.ds(pl.multiple_of(r0 + half * c, c), c)
            gc = jnp.dot(tri_incl, gb[half * c:(half + 1) * c, :], precision=lax.Precision.HIGHEST,
                         preferred_element_type=F32)
            gcs[hs, :] = gc
            gct[hs, :] = gc.T

    prep(0, True)
    lax.fori_loop(1, seq // rows, lambda i, carry: (prep(i, False), carry)[1], 0)

    hd = HEAD_DIM
    left = lax.broadcasted_iota(jnp.int32, (c, 2 * c), 1) < c
    causal2 = jnp.concatenate([col_i <= row_i, col_i <= row_i], axis=1)
    strict2 = jnp.concatenate([col_i < row_i, col_i < row_i], axis=1)
    eye2 = jnp.concatenate([eye, eye], axis=1)

    def pair_diag(x2):
        return jnp.concatenate([jnp.where(left, x2, 0.0), jnp.where(left, 0.0, x2)], axis=0)

    def independent(qa, ka, va, ba, gca, gra, qb, kb_, vb, bb, gcb, grb):
        lanes = lambda xa, xb: jnp.concatenate([xa, xb], axis=1)
        q2, k2, beta2, gc2 = lanes(qa, qb), lanes(ka, kb_), lanes(ba, bb), lanes(gca, gcb)
        decay2 = jnp.exp(jnp.where(causal2, gc2 - lanes(gra, grb), -jnp.inf))
        kbeta2 = k2 * beta2
        kq = _nt(jnp.concatenate([kbeta2, q2], axis=0), pair_diag(k2))
        nm = kq[:c, :] * jnp.where(strict2, decay2, 0.0)
        attn2 = kq[c:, :] * decay2
        inv = eye2 - nm
        pw = _nn(nm, pair_diag(nm))
        for lvl in range(6):
            if lvl < 5:
                both = _nn(jnp.concatenate([inv, pw], axis=0), pair_diag(pw))
                inv, pw = inv + both[:c, :], both[c:, :]
            else:
                inv = inv + _nn(inv, pair_diag(pw))
        egc2 = jnp.exp(gc2)
        vbeta2, kbe2, qd2 = lanes(va, vb) * beta2, kbeta2 * egc2, q2 * egc2
        rhs_a = jnp.concatenate([vbeta2[:, :hd], kbe2[:, :hd]], axis=1)
        rhs_b = jnp.concatenate([vbeta2[:, hd:], kbe2[:, hd:]], axis=1)
        zero = jnp.zeros_like(rhs_a)
        uw = _nn(inv, jnp.concatenate([jnp.concatenate([rhs_a, zero], axis=1),
                                       jnp.concatenate([zero, rhs_b], axis=1)], axis=0))
        out = []
        for h, (k, gc) in enumerate(((ka, gca), (kb_, gcb))):
            lo = h * 2 * hd
            kdt = (k * jnp.exp(gc[c - 1:c, :] - gc)).T
            out.append((uw[:, lo:lo + hd],
                        jnp.concatenate([uw[:, lo + hd:lo + 2 * hd], qd2[:, h * hd:(h + 1) * hd]],
                                        axis=0).astype(BF16),
                        jnp.concatenate([attn2[:, h * hd:(h + 1) * hd], kdt], axis=0).astype(BF16)))
        return out

    group = GDN_INTERLEAVE
    assert group % 2 == 0

    def independent_group(i, carry):
        sls = [pl.ds(pl.multiple_of((i * group + u) * c, c), c) for u in range(group)]
        s2s = [pl.ds(pl.multiple_of((i * group + u) * (2 * c), 2 * c), 2 * c) for u in range(group)]
        loaded = [tuple(ref[sl, :] for ref in (qn, kn, vn, bts, gcs, gct)) for sl in sls]
        results = []
        for u in range(0, group, 2):
            results += independent(*loaded[u], *loaded[u + 1])
        for sl, s2, (u_new, wq_new, ak_new) in zip(sls, s2s, results):
            us[sl, :] = u_new
            wq[s2, :] = wq_new
            ak[s2, :] = ak_new
        return carry

    lax.fori_loop(0, seq // (c * group), independent_group, 0)

    def scan(n, state):
        sl = pl.ds(pl.multiple_of(n * c, c), c)
        s2 = pl.ds(pl.multiple_of(n * (2 * c), 2 * c), 2 * c)
        ws_qs = _nn(wq[s2, :], state)
        v_new = us[sl, :] - ws_qs[:c, :]
        av_kv = _nn(ak[s2, :], v_new)
        o = ws_qs[c:, :] + av_kv[:c, :]
        last = jnp.exp(gcs[pl.ds(n * c + c - 1, 1), :])
        o = o * lax.rsqrt(jnp.mean(o * o, axis=-1, keepdims=True) + NORM_EPS) * nw_ref[...]
        gate = gate_ref[sl, :]
        o_ref[sl, :] = (o * (gate * (1.0 / (1.0 + jnp.exp(-gate))))).astype(o_ref.dtype)
        return state * last + av_kv[c:, :]

    lax.fori_loop(0, seq // c, scan, jnp.zeros((HEAD_DIM, HEAD_DIM), F32))


def gated_delta_net(h, ab, conv_w, a_log, dt_bias, norm_w, *, q_col, k_col, v_col, gate_col):
    b, seq, _ = h.shape
    nh = GDN_HEADS
    assert seq % (max(2, GDN_INTERLEAVE) * GDN_CHUNK) == 0
    col = lambda c0: pl.BlockSpec((None, seq, HEAD_DIM), lambda bi, hi, *_: (bi, 0, c0 + hi))
    cw = lambda c0: pl.BlockSpec((GDN_CONV, HEAD_DIM), lambda bi, hi, *_: (0, c0 + hi))
    f32s = lambda: pltpu.VMEM((seq, HEAD_DIM), F32)
    b16s = lambda: pltpu.VMEM((2 * seq, HEAD_DIM), BF16)
    grid_spec = pltpu.PrefetchScalarGridSpec(
        num_scalar_prefetch=2,
        grid=(b, nh),
        in_specs=[col(q_col), col(k_col), col(v_col), col(gate_col), cw(0), cw(nh), cw(2 * nh),
                  pl.BlockSpec((None, seq, 2 * nh), lambda bi, hi, *_: (bi, 0, 0)),
                  pl.BlockSpec((1, HEAD_DIM), lambda bi, hi, *_: (0, 0))],
        out_specs=pl.BlockSpec((None, seq, HEAD_DIM), lambda bi, hi, *_: (bi, 0, hi)),
        scratch_shapes=[f32s(), f32s(), f32s(), f32s(), f32s(), f32s(), f32s(), b16s(), b16s()],
    )
    return pl.pallas_call(
        functools.partial(_gdn_kernel, seq=seq),
        out_shape=jax.ShapeDtypeStruct((b, seq, nh * HEAD_DIM), BF16),
        grid_spec=grid_spec,
        compiler_params=_params("arbitrary", "arbitrary"),
        name="gdn",
    )(a_log, dt_bias, h, h, h, h, conv_w, conv_w, conv_w, ab, norm_w.reshape(1, HEAD_DIM))


def _fox_cum_kernel(fl_ref, bf_ref, o_ref, *, seq):
    c = 128
    tri_incl = (lax.broadcasted_iota(jnp.int32, (c, c), 1)
                <= lax.broadcasted_iota(jnp.int32, (c, c), 0)).astype(F32)

    def body(n, carry):
        sl = pl.ds(pl.multiple_of(n * c, c), c)
        z = fl_ref[sl, :] + bf_ref[...]
        log_f = -(jnp.maximum(-z, 0.0) + jnp.log(1.0 + jnp.exp(-jnp.abs(z))))
        cs = jnp.dot(tri_incl, log_f, precision=lax.Precision.HIGHEST,
                     preferred_element_type=F32) + carry
        o_ref[sl, :] = cs * LOG2E
        return cs[c - 1:c, :]

    lax.fori_loop(0, seq // c, body, jnp.zeros((1, fl_ref.shape[-1]), F32))


def forget_gate_cumsum(f_logit, b_f):
    b, seq, nh = f_logit.shape
    blk = pl.BlockSpec((None, seq, nh), lambda bi: (bi, 0, 0))
    return pl.pallas_call(
        functools.partial(_fox_cum_kernel, seq=seq),
        out_shape=jax.ShapeDtypeStruct((b, seq, nh), F32),
        grid=(b,),
        in_specs=[blk, pl.BlockSpec((1, nh), lambda bi: (0, 0))],
        out_specs=blk,
        compiler_params=_params("arbitrary"),
        name="fox_cumsum",
    )(f_logit, b_f.reshape(1, nh))


def _fox_kernel(q_ref, k_ref, v_ref, cum_ref, o_ref, *, blk, heads):
    qi = pl.program_id(2)
    hd = HEAD_DIM
    ri = lax.broadcasted_iota(jnp.int32, (blk, blk), 0)
    ci = lax.broadcasted_iota(jnp.int32, (blk, blk), 1)
    eye = (lax.broadcasted_iota(jnp.int32, (hd, hd), 0)
           == lax.broadcasted_iota(jnp.int32, (hd, hd), 1))
    qs = pl.ds(pl.multiple_of(qi * blk, blk), blk)

    def as_column(row):
        return jnp.concatenate(
            [jnp.sum(jnp.where(eye, row[:, c * hd:(c + 1) * hd], 0.0), axis=1, keepdims=True)
             for c in range(blk // hd)], axis=0)

    cq = [as_column(cum_ref[g:g + 1, qs]) for g in range(heads)]
    q = [q_ref[:, g * hd:(g + 1) * hd] for g in range(heads)]

    def step(j, carry, diagonal):
        ks = pl.ds(pl.multiple_of(j * blk, blk), blk)
        out = []
        for g in range(heads):
            m, l, acc = carry[g]
            s = _nt(q[g], k_ref[ks, g * hd:(g + 1) * hd]) + (cq[g] - cum_ref[g:g + 1, ks])
            if diagonal:
                s = jnp.where(ri >= ci, s, NEG_INF)
            m_new = jnp.maximum(m, jnp.max(s, axis=1, keepdims=True))
            alpha = jnp.exp2(m - m_new)
            p = jnp.exp2(s - m_new)
            l = alpha * l + jnp.sum(p, axis=1, keepdims=True)
            acc = alpha * acc + jnp.dot(p.astype(BF16), v_ref[ks, g * hd:(g + 1) * hd],
                                        preferred_element_type=F32)
            out.append((m_new, l, acc))
        return tuple(out)

    init = tuple((jnp.full((blk, 1), NEG_INF, F32), jnp.zeros((blk, 1), F32),
                  jnp.zeros((blk, hd), F32)) for _ in range(heads))
    carry = lax.fori_loop(0, qi, lambda j, cr: step(j, cr, False), init)
    carry = step(qi, carry, True)
    for g in range(heads):
        _, l, acc = carry[g]
        o_ref[:, g * hd:(g + 1) * hd] = (acc / l).astype(o_ref.dtype)


def forgetting_attention(hq, cum_rows, *, n_heads, blk=512, heads=2):
    b, seq, _ = hq.shape
    assert seq % blk == 0 and n_heads % heads == 0
    w = heads * HEAD_DIM
    ng = n_heads // heads
    kv = lambda c0: pl.BlockSpec((None, seq, w), lambda bi, hi, qi: (bi, 0, c0 + hi))
    return pl.pallas_call(
        functools.partial(_fox_kernel, blk=blk, heads=heads),
        out_shape=jax.ShapeDtypeStruct((b, seq, n_heads * HEAD_DIM), BF16),
        grid=(b, ng, seq // blk),
        in_specs=[pl.BlockSpec((None, blk, w), lambda bi, hi, qi: (bi, qi, hi)),
                  kv(ng), kv(2 * ng),
                  pl.BlockSpec((None, None, heads, seq), lambda bi, hi, qi: (bi, hi, 0, 0))],
        out_specs=pl.BlockSpec((None, blk, w), lambda bi, hi, qi: (bi, qi, hi)),
        compiler_params=_params("arbitrary", "arbitrary", "arbitrary"),
        name="fox",
    )(hq, hq, hq, cum_rows)


def _router_kernel(x_ref, rw_ref, rb_ref, idx_ref, gate_ref):
    logits = lax.dot_general(rw_ref[...], x_ref[...], (((1,), (1,)), ((), ())),
                             precision=lax.Precision.HIGHEST, preferred_element_type=F32)
    score = 1.0 / (1.0 + jnp.exp(-logits))
    sel = score + rb_ref[...]
    sel_r = [sel[e:e + 1, :] for e in range(N_EXPERTS)]
    score_r = [score[e:e + 1, :] for e in range(N_EXPERTS)]

    def top2_sum(a, b, c, d):
        hi1, lo1, hi2, lo2 = jnp.maximum(a, b), jnp.minimum(a, b), jnp.maximum(c, d), jnp.minimum(c, d)
        return jnp.maximum(hi1, hi2) + jnp.maximum(jnp.minimum(hi1, hi2), jnp.maximum(lo1, lo2))

    epg = EXPERTS_PER_GROUP
    best = top2_sum(*sel_r[:epg])
    grp = jnp.zeros_like(best, dtype=jnp.int32)
    for g in range(1, N_GROUPS):
        cand = top2_sum(*sel_r[g * epg:(g + 1) * epg])
        take = cand > best
        best = jnp.where(take, cand, best)
        grp = jnp.where(take, g, grp)
    in_sel, in_score = [], []
    for j in range(epg):
        s_j, u_j = sel_r[j], score_r[j]
        for g in range(1, N_GROUPS):
            s_j = jnp.where(grp == g, sel_r[g * epg + j], s_j)
            u_j = jnp.where(grp == g, score_r[g * epg + j], u_j)
        in_sel.append(s_j)
        in_score.append(u_j)

    def arg_best(vals):
        bv, bi = vals[0], jnp.zeros_like(grp)
        for j in range(1, epg):
            take = vals[j] > bv
            bv = jnp.where(take, vals[j], bv)
            bi = jnp.where(take, j, bi)
        return bi

    i1 = arg_best(in_sel)
    i2 = arg_best([jnp.where(i1 == j, -jnp.inf, in_sel[j]) for j in range(epg)])

    def pick(vals, i):
        out = vals[0]
        for j in range(1, epg):
            out = jnp.where(i == j, vals[j], out)
        return out

    g1, g2 = pick(in_score, i1), pick(in_score, i2)
    tot = g1 + g2
    idx_ref[0:1, :] = grp * epg + i1
    idx_ref[1:2, :] = grp * epg + i2
    gate_ref[0:1, :] = g1 / tot
    gate_ref[1:2, :] = g2 / tot


def route(x, router_w, router_bias, *, tm=512):
    t, d = x.shape
    tm = min(tm, t)
    out = pl.BlockSpec((2, tm), lambda i: (0, i))
    return pl.pallas_call(
        _router_kernel,
        out_shape=(jax.ShapeDtypeStruct((2, t), jnp.int32), jax.ShapeDtypeStruct((2, t), F32)),
        grid=(t // tm,),
        in_specs=[pl.BlockSpec((tm, d), lambda i: (i, 0)),
                  pl.BlockSpec((N_EXPERTS, d), lambda i: (0, 0)),
                  pl.BlockSpec((N_EXPERTS, 1), lambda i: (0, 0))],
        out_specs=(out, out),
        compiler_params=_params("arbitrary"),
        name="router",
    )(x, router_w.T, router_bias.reshape(N_EXPERTS, 1))


def _moe_up_kernel(be_ref, first_ref, nused_ref, x_ref, wg_ref, wu_ref, o_ref, wgb, wub):
    i = pl.program_id(1)

    @pl.when(i < nused_ref[0])
    def _():
        @pl.when(first_ref[i] == 1)
        def _():
            wgb[...] = wg_ref[...].astype(BF16)
            wub[...] = wu_ref[...].astype(BF16)

        xs = x_ref[...]
        a = jnp.dot(xs, wgb[...], preferred_element_type=F32)
        u = jnp.dot(xs, wub[...], preferred_element_type=F32)
        o_ref[...] = (a * (1.0 / (1.0 + jnp.exp(-a))) * u).astype(o_ref.dtype)

    @pl.when(i >= nused_ref[0])
    def _():
        o_ref[...] = jnp.zeros_like(o_ref)


def _moe_down_kernel(be_ref, nused_ref, h_ref, wd_ref, o_ref, *, rows, tn, d_chunks):
    i, j = pl.program_id(0), pl.program_id(1)
    n_chunks = tn // V7X_LANES
    used = i < nused_ref[0]
    y = jnp.where(used, jnp.dot(h_ref[...], wd_ref[...].astype(BF16), preferred_element_type=F32), 0.0)
    for cc in range(n_chunks):
        o_ref[pl.ds(j * n_chunks + cc, rows, stride=ROW_STRIDE), :] = (
            y[:, cc * V7X_LANES:(cc + 1) * V7X_LANES])

    @pl.when(j == 0)
    def _():
        for c in range(d_chunks, ROW_STRIDE):
            o_ref[pl.ds(c, rows, stride=ROW_STRIDE), :] = jnp.zeros((rows, V7X_LANES), F32)


def expert_ffn(xg, block_expert, block_first, n_used, w_gate, w_up, w_down, layer, *, tn_up=384,
               tn_down=1024):
    r, d = xg.shape
    de = w_gate.shape[-1]
    nb = r // MOE_ROWS
    w_in = pl.BlockSpec((None, None, d, tn_up), lambda j, i, be, fi, nu: (layer, be[i], 0, j))
    hid = pl.pallas_call(
        _moe_up_kernel,
        out_shape=jax.ShapeDtypeStruct((r, de), BF16),
        grid_spec=pltpu.PrefetchScalarGridSpec(
            num_scalar_prefetch=3,
            grid=(de // tn_up, nb),
            in_specs=[pl.BlockSpec((MOE_ROWS, d), lambda j, i, *_: (i, 0)), w_in, w_in],
            out_specs=pl.BlockSpec((MOE_ROWS, tn_up), lambda j, i, *_: (i, j)),
            scratch_shapes=[pltpu.VMEM((d, tn_up), BF16), pltpu.VMEM((d, tn_up), BF16)],
        ),
        compiler_params=_params("arbitrary", "arbitrary"),
        name="moe_up",
    )(block_expert, block_first, n_used, xg, w_gate, w_up)
    d_chunks = d // V7X_LANES
    assert d_chunks <= ROW_STRIDE
    return pl.pallas_call(
        functools.partial(_moe_down_kernel, rows=MOE_ROWS, tn=tn_down, d_chunks=d_chunks),
        out_shape=jax.ShapeDtypeStruct((r * ROW_STRIDE, V7X_LANES), F32),
        grid_spec=pltpu.PrefetchScalarGridSpec(
            num_scalar_prefetch=2,
            grid=(nb, d // tn_down),
            in_specs=[pl.BlockSpec((MOE_ROWS, de), lambda i, j, *_: (i, 0)),
                      pl.BlockSpec((None, None, de, tn_down),
                                   lambda i, j, be, nu: (layer, be[i], 0, j))],
            out_specs=pl.BlockSpec((MOE_ROWS * ROW_STRIDE, V7X_LANES), lambda i, j, *_: (i, 0)),
        ),
        compiler_params=_params("arbitrary", "arbitrary"),
        name="moe_down",
    )(block_expert, n_used, hid, w_down)


DMA_PRIORITIES = 2
DMA_BURST = 8


def _token_row_copy(src_ref, src_tok, dst_ref, dst_tok, n_chunks, sem):
    src = pl.ds(pl.multiple_of(src_tok * ROW_STRIDE, ROW_STRIDE), n_chunks)
    dst = pl.ds(pl.multiple_of(dst_tok * ROW_STRIDE, ROW_STRIDE), n_chunks)
    return pltpu.make_async_copy(src_ref.at[src], dst_ref.at[dst], sem)


def _start_token_gather(idx_of, src_ref, buf_ref, n, n_chunks, sem):
    assert n % DMA_BURST == 0

    def issue(g, carry):
        for u in range(DMA_BURST):
            r = g * DMA_BURST + u
            _token_row_copy(src_ref, idx_of(r), buf_ref, r, n_chunks, sem).start(
                priority=u % DMA_PRIORITIES)
        return carry

    lax.fori_loop(0, n // DMA_BURST, issue, 0)


def _wait_token_gather(src_ref, buf_ref, n, n_chunks, sem):
    def drain(g, carry):
        for u in range(DMA_BURST):
            _token_row_copy(src_ref, 0, buf_ref, g * DMA_BURST + u, n_chunks, sem).wait()
        return carry

    lax.fori_loop(0, n // DMA_BURST, drain, 0)


def _gather_kernel(idx_ref, nused_ref, src_ref, o_ref, buf, sems, *, rows):
    i = pl.program_id(0)
    n_used = nused_ref[0]
    n_chunks = o_ref.shape[1] // V7X_LANES
    slot = i % 2

    def start(block, s):
        _start_token_gather(lambda r: idx_ref[block * rows + r], src_ref, buf.at[s], rows, n_chunks,
                            sems.at[s])

    @pl.when((i == 0) & (n_used > 0))
    def _():
        start(0, 0)

    @pl.when(i + 1 < n_used)
    def _():
        start(i + 1, 1 - slot)

    @pl.when(i < n_used)
    def _():
        _wait_token_gather(src_ref, buf.at[slot], rows, n_chunks, sems.at[slot])
        for c in range(n_chunks):
            o_ref[:, c * V7X_LANES:(c + 1) * V7X_LANES] = _token_chunk(buf.at[slot], c, rows).astype(BF16)

    @pl.when(i >= n_used)
    def _():
        o_ref[...] = jnp.zeros_like(o_ref)


def gather_rows(src_rows, idx, n_used, d, *, rows=MOE_ROWS):
    m = idx.shape[0]
    assert m % rows == 0
    return pl.pallas_call(
        functools.partial(_gather_kernel, rows=rows),
        out_shape=jax.ShapeDtypeStruct((m, d), BF16),
        grid_spec=pltpu.PrefetchScalarGridSpec(
            num_scalar_prefetch=2,
            grid=(m // rows,),
            in_specs=[pl.BlockSpec(memory_space=pl.ANY)],
            out_specs=pl.BlockSpec((rows, d), lambda i, *_: (i, 0)),
            scratch_shapes=[pltpu.VMEM((2, rows * ROW_STRIDE, V7X_LANES), F32),
                            pltpu.SemaphoreType.DMA((2,))],
        ),
        compiler_params=_params("arbitrary"),
        name="gather_rows",
    )(idx, n_used, src_rows)


def _combine_ln_kernel(pos_ref, x_ref, y_ref, gate_ref, g_ref, b_ref, o_ref, ob_ref, ya, yb, sems,
                       *, tm, n_tok):
    i = pl.program_id(0)
    n_chunks = x_ref.shape[1] // V7X_LANES
    slot = i % 2

    def start(tile, s):
        _start_token_gather(lambda r: pos_ref[tile * tm + r], y_ref, ya.at[s], tm, n_chunks, sems.at[s])
        _start_token_gather(lambda r: pos_ref[n_tok + tile * tm + r], y_ref, yb.at[s], tm, n_chunks,
                            sems.at[s])

    @pl.when(i == 0)
    def _():
        start(0, 0)

    @pl.when(i + 1 < pl.num_programs(0))
    def _():
        start(i + 1, 1 - slot)

    _wait_token_gather(y_ref, ya.at[slot], tm, n_chunks, sems.at[slot])
    _wait_token_gather(y_ref, yb.at[slot], tm, n_chunks, sems.at[slot])
    g0, g1 = gate_ref[:, 0:1], gate_ref[:, 1:2]
    ffn = jnp.concatenate([_token_chunk(ya.at[slot], c, tm) * g0 + _token_chunk(yb.at[slot], c, tm) * g1
                           for c in range(n_chunks)], axis=1)
    out = _layer_norm_rows(DEEPNORM_ALPHA * x_ref[...] + ffn, g_ref, b_ref)
    o_ref[...] = out
    ob_ref[...] = out.astype(BF16)


def moe_combine_layer_norm(x, y_rows, pos, gate, g, b, layer, *, tm=256):
    t, d = x.shape
    tm = min(tm, t)
    row = pl.BlockSpec((tm, d), lambda i, *_: (i, 0))
    par = pl.BlockSpec((None, 1, d), lambda i, *_: (layer, 0, 0))
    buf = lambda: pltpu.VMEM((2, tm * ROW_STRIDE, V7X_LANES), F32)
    return pl.pallas_call(
        functools.partial(_combine_ln_kernel, tm=tm, n_tok=t),
        out_shape=(jax.ShapeDtypeStruct((t, d), F32), jax.ShapeDtypeStruct((t, d), BF16)),
        grid_spec=pltpu.PrefetchScalarGridSpec(
            num_scalar_prefetch=1,
            grid=(t // tm,),
            in_specs=[row, pl.BlockSpec(memory_space=pl.ANY),
                      pl.BlockSpec((tm, 2), lambda i, *_: (i, 0)), par, par],
            out_specs=(row, row),
            scratch_shapes=[buf(), buf(), pltpu.SemaphoreType.DMA((2,))],
        ),
        compiler_params=_params("arbitrary"),
        name="moe_combine_ln",
    )(pos, x, y_rows, gate, g.reshape(-1, 1, d), b.reshape(-1, 1, d))


def moe_ffn(x, x_rows, router_w, router_bias, w_gate, w_up, w_down, layer):
    t, d = x.shape
    idx, gate = route(x, router_w, router_bias)
    flat_e = idx.reshape(-1)
    n_assign = flat_e.shape[0]
    order = jnp.argsort(flat_e).astype(jnp.int32)
    e_sorted = flat_e[order]
    bounds = jnp.searchsorted(e_sorted, jnp.arange(N_EXPERTS + 1, dtype=jnp.int32)).astype(jnp.int32)
    start, counts = bounds[:-1], bounds[1:] - bounds[:-1]
    padded = (counts + MOE_ROWS - 1) // MOE_ROWS * MOE_ROWS
    pad_end = jnp.cumsum(padded)
    pad_start = pad_end - padded
    dest_sorted = pad_start[e_sorted] + jnp.arange(n_assign, dtype=jnp.int32) - start[e_sorted]
    n_blocks = -(-(n_assign + N_EXPERTS * (MOE_ROWS - 1)) // MOE_ROWS)
    rows = n_blocks * MOE_ROWS
    blk_start = jnp.arange(n_blocks, dtype=jnp.int32) * MOE_ROWS
    block_expert = jnp.minimum(jnp.searchsorted(pad_end, blk_start, side='right'),
                               N_EXPERTS - 1).astype(jnp.int32)
    n_used = (pad_end[-1] // MOE_ROWS).astype(jnp.int32).reshape(1)
    block_first = jnp.concatenate([jnp.ones((1,), jnp.int32),
                                   (block_expert[1:] != block_expert[:-1]).astype(jnp.int32)])
    tok_sorted = order % t
    row_tok = (jnp.arange(rows, dtype=jnp.int32) % t).at[dest_sorted].set(tok_sorted)
    pos = jnp.zeros((n_assign,), jnp.int32).at[order].set(dest_sorted)
    xg = gather_rows(x_rows, row_tok, n_used, d)
    y_rows = expert_ffn(xg, block_expert, block_first, n_used, w_gate, w_up, w_down, layer)
    return y_rows, pos, gate.T


def _q_scale(n_q, n_total):
    return jnp.concatenate([jnp.full((1, n_q), ATTN_Q_SCALE, F32),
                            jnp.ones((1, n_total - n_q), F32)], axis=1)


def kernel(x, w_in_even, conv_w, a_log, dt_bias, gdn_norm_w, w_out_even, w_in_odd, b_f, w_out_odd, ln_mix_g, ln_mix_b, ln_ffn_g, ln_ffn_b, w_gate, w_up, w_down, router_w, router_bias):
    b, seq, d = x.shape
    t = b * seq
    xf = x.reshape(t, d)
    xb = xf.astype(BF16)
    tables = rotary_tables(seq)
    hb = HEAD_DIM
    for layer in range(DEPTH):
        i = layer // 2
        if layer % 2 == 0:
            wa, wqkv, wg = DSWA_HEADS * hb, 3 * GDN_HEADS * hb, GDN_HEADS * hb
            n_main = 3 * wa + wqkv + wg
            w_in = w_in_even[i].astype(BF16)
            h = matmul(xb, w_in, ncols=n_main, col_scale=_q_scale(wa, n_main),
                       tn=IN_PROJ_TN).reshape(b, seq, n_main)
            ab = matmul(xb, w_in[:, n_main:]).reshape(b, seq, 2 * GDN_HEADS)
            o_a = dilated_mixture_attention(h, tables, q_col=0, k_col=DSWA_HEADS,
                                            v_col=2 * DSWA_HEADS, n_heads=DSWA_HEADS)
            c0 = 3 * DSWA_HEADS
            o_b = gated_delta_net(h, ab, conv_w[i], a_log[i], dt_bias[i], gdn_norm_w[i],
                                  q_col=c0, k_col=c0 + GDN_HEADS, v_col=c0 + 2 * GDN_HEADS,
                                  gate_col=c0 + 3 * GDN_HEADS)
            o = jnp.concatenate([o_a, o_b], axis=-1).reshape(t, -1)
            mix = matmul(o, w_out_even, lead=i)
        else:
            n_main = 3 * FOX_HEADS * hb
            w_in = w_in_odd[i].astype(BF16)
            hq = matmul(xb, w_in, ncols=n_main, col_scale=_q_scale(FOX_HEADS * hb, n_main),
                        out_dtype=BF16, tn=IN_PROJ_TN).reshape(b, seq, n_main)
            f_logit = matmul(xb, w_in[:, n_main:]).reshape(b, seq, FOX_HEADS)
            cum = forget_gate_cumsum(f_logit, b_f[i])
            pair = 2
            cum_rows = cum.transpose(0, 2, 1).reshape(b, FOX_HEADS // pair, pair, seq)
            o = forgetting_attention(hq, cum_rows, n_heads=FOX_HEADS, heads=pair).reshape(t, -1)
            mix = matmul(o, w_out_odd, lead=i)
        xf, _, x_rows = residual_layer_norm(xf, mix, ln_mix_g, ln_mix_b, layer, token_rows=True)
        y_rows, pos, gate = moe_ffn(xf, x_rows, router_w, router_bias, w_gate, w_up, w_down, layer)
        xf, xb = moe_combine_layer_norm(xf, y_rows, pos, gate, ln_ffn_g, ln_ffn_b, layer)
    return xf.reshape(b, seq, d)
```

```python
import functools

import jax
import jax.numpy as jnp
import numpy as np
from jax import lax
from jax.experimental import pallas as pl
from jax.experimental.pallas import tpu as pltpu

F32 = jnp.float32
BF16 = jnp.bfloat16

V7X_LANES = 128
V7X_VMEM_BYTES = 64 * 1024 * 1024
VMEM_LIMIT = 56 * 1024 * 1024

HEAD_DIM = 128
DSWA_HEADS = 16
DSWA_DILATIONS = (1, 4, 16)
BAND = 128
DSWA_QUERY_BLOCK = 256
GDN_HEADS = 16
GDN_CONV = 4
GDN_CHUNK = 128
GDN_INTERLEAVE = 4
FOX_HEADS = 32
ROPE_THETA = 500000.0
ROPE_DIM = HEAD_DIM // 4
N_EXPERTS = 32
N_GROUPS = 8
EXPERTS_PER_GROUP = 4
D_EXPERT = 768
MOE_ROWS = 512
LN_EPS = 1e-5
NORM_EPS = 1e-6
NEG_INF = -1e30
DEPTH = 2
DEEPNORM_ALPHA = (2 * DEPTH) ** 0.25
LOG2E = 1.4426950408889634
ATTN_Q_SCALE = HEAD_DIM ** -0.5 * LOG2E


def _params(*sem):
    return pltpu.CompilerParams(dimension_semantics=sem, vmem_limit_bytes=VMEM_LIMIT)


def _mm_kernel(x_ref, w_ref, s_ref, o_ref, *scratch):
    if scratch:
        wb_ref, = scratch

        @pl.when(pl.program_id(1) == 0)
        def _():
            wb_ref[...] = w_ref[...].astype(BF16)
    else:
        wb_ref = w_ref
    acc = jnp.dot(x_ref[...], wb_ref[...], preferred_element_type=F32)
    o_ref[...] = (acc * s_ref[...]).astype(o_ref.dtype)


def matmul(x, w, *, lead=None, col0=0, ncols=None, col_scale=None, out_dtype=F32, tm=1024, tn=512):
    m, k = x.shape
    n_total = w.shape[-1]
    ncols = n_total - col0 if ncols is None else ncols
    tn = min(tn, ncols)
    tm = min(tm, m)
    assert ncols % tn == 0 and col0 % tn == 0 and m % tm == 0
    cb = col0 // tn
    if lead is None:
        w_spec = pl.BlockSpec((k, tn), lambda j, i: (0, j + cb))
    else:
        w_spec = pl.BlockSpec((None, k, tn), lambda j, i: (lead, 0, j + cb))
    if col_scale is None:
        col_scale = jnp.ones((1, ncols), F32)
    return pl.pallas_call(
        _mm_kernel,
        out_shape=jax.ShapeDtypeStruct((m, ncols), out_dtype),
        grid=(ncols // tn, m // tm),
        in_specs=[pl.BlockSpec((tm, k), lambda j, i: (i, 0)), w_spec,
                  pl.BlockSpec((1, tn), lambda j, i: (0, j))],
        out_specs=pl.BlockSpec((tm, tn), lambda j, i: (i, j)),
        scratch_shapes=[] if w.dtype == BF16 else [pltpu.VMEM((k, tn), BF16)],
        compiler_params=_params("arbitrary", "arbitrary"),
        name="matmul",
    )(x, w, col_scale)


def _mm_nt_kernel(x_ref, wt_ref, s_ref, o_ref, wb_ref):
    @pl.when(pl.program_id(1) == 0)
    def _():
        wb_ref[...] = wt_ref[...].astype(BF16)

    acc = lax.dot_general(x_ref[...], wb_ref[...], (((1,), (1,)), ((), ())),
                          preferred_element_type=F32)
    o_ref[...] = (acc * s_ref[...]).astype(o_ref.dtype)


def matmul_nt(x, wt, *, row0=0, nrows=None, col_scale=None, out_dtype=F32, tm=1024, tn=512):
    m, k = x.shape
    nrows = wt.shape[0] - row0 if nrows is None else nrows
    tn = min(tn, nrows)
    tm = min(tm, m)
    assert nrows % tn == 0 and row0 % tn == 0 and m % tm == 0
    rb = row0 // tn
    if col_scale is None:
        col_scale = jnp.ones((1, nrows), F32)
    return pl.pallas_call(
        _mm_nt_kernel,
        out_shape=jax.ShapeDtypeStruct((m, nrows), out_dtype),
        grid=(nrows // tn, m // tm),
        in_specs=[pl.BlockSpec((tm, k), lambda j, i: (i, 0)),
                  pl.BlockSpec((tn, k), lambda j, i: (j + rb, 0)),
                  pl.BlockSpec((1, tn), lambda j, i: (0, j))],
        out_specs=pl.BlockSpec((tm, tn), lambda j, i: (i, j)),
        scratch_shapes=[pltpu.VMEM((tn, k), BF16)],
        compiler_params=_params("arbitrary", "arbitrary"),
        name="matmul_nt",
    )(x, wt, col_scale)


ROW_STRIDE = 40


def _store_token_rows(dst_ref, val):
    n, d = val.shape
    for c in range(ROW_STRIDE):
        chunk = val[:, c * V7X_LANES:(c + 1) * V7X_LANES] if c < d // V7X_LANES else jnp.zeros(
            (n, V7X_LANES), F32)
        dst_ref[pl.ds(c, n, stride=ROW_STRIDE), :] = chunk


def _token_chunk(src_ref, c, n):
    return src_ref[pl.ds(c, n, stride=ROW_STRIDE), :]


def _layer_norm_rows(y, g_ref, b_ref):
    mu = jnp.mean(y, axis=-1, keepdims=True)
    yc = y - mu
    var = jnp.mean(yc * yc, axis=-1, keepdims=True)
    return yc * lax.rsqrt(var + LN_EPS) * g_ref[...] + b_ref[...]


def _ln_kernel(x_ref, mix_ref, g_ref, b_ref, o_ref, rows_ref):
    out = _layer_norm_rows(DEEPNORM_ALPHA * x_ref[...] + mix_ref[...], g_ref, b_ref)
    o_ref[...] = out
    _store_token_rows(rows_ref, out)


def residual_layer_norm(x, mix, g, b, layer, *, tm=256):
    m, d = x.shape
    tm = min(tm, m)
    row = pl.BlockSpec((tm, d), lambda i: (i, 0))
    par = pl.BlockSpec((None, 1, d), lambda i: (layer, 0, 0))
    return pl.pallas_call(
        _ln_kernel,
        out_shape=(jax.ShapeDtypeStruct((m, d), F32),
                   jax.ShapeDtypeStruct((m * ROW_STRIDE, V7X_LANES), F32)),
        grid=(m // tm,),
        in_specs=[row, row, par, par],
        out_specs=(row, pl.BlockSpec((tm * ROW_STRIDE, V7X_LANES), lambda i: (i, 0))),
        compiler_params=_params("arbitrary"),
        name="residual_ln",
    )(x, mix, g.reshape(-1, 1, d), b.reshape(-1, 1, d))


def rotary_tables(seq):
    half = ROPE_DIM // 2
    inv_freq = ROPE_THETA ** (-jnp.arange(half, dtype=F32) / half)
    ang = jnp.arange(seq, dtype=F32)[:, None] * inv_freq[None, :]
    cos, sin = jnp.cos(ang), jnp.sin(ang)
    ones = jnp.ones((seq, HEAD_DIM - ROPE_DIM), F32)
    zeros = jnp.zeros((seq, HEAD_DIM - ROPE_DIM), F32)
    zh = jnp.zeros((seq, half), F32)
    c = jnp.concatenate([cos, cos, ones], axis=-1)
    s_lo = jnp.concatenate([-sin, zh, zeros], axis=-1)
    s_hi = jnp.concatenate([zh, sin, zeros], axis=-1)
    return c, s_lo, s_hi


def _dswa_kernel(q_ref, k_ref, v_ref, c_ref, slo_ref, shi_ref, o_ref, qs, ks, acc, ms, ls, *, seq):
    half = ROPE_DIM // 2
    rows = 256

    def prep(i, carry):
        sl = pl.ds(pl.multiple_of(i * rows, rows), rows)
        c, slo, shi = c_ref[sl, :], slo_ref[sl, :], shi_ref[sl, :]
        for src, dst in ((q_ref, qs), (k_ref, ks)):
            x = src[sl, :]
            dst[sl, :] = (x * c + pltpu.roll(x, HEAD_DIM - half, 1) * slo
                          + pltpu.roll(x, half, 1) * shi)
        acc[sl, :] = jnp.zeros((rows, HEAD_DIM), F32)
        ls[sl, :] = jnp.zeros((rows, HEAD_DIM), F32)
        ms[sl, :] = jnp.full((rows, HEAD_DIM), NEG_INF, F32)
        return carry

    lax.fori_loop(0, seq // rows, prep, 0)

    qb = DSWA_QUERY_BLOCK
    qi = lax.broadcasted_iota(jnp.int32, (qb, BAND + qb), 0)
    kj = lax.broadcasted_iota(jnp.int32, (qb, BAND + qb), 1)
    band_mask = (kj >= qi) & (kj <= qi + BAND)
    qi0 = lax.broadcasted_iota(jnp.int32, (qb, qb), 0)
    kj0 = lax.broadcasted_iota(jnp.int32, (qb, qb), 1)
    first_mask = (kj0 <= qi0) & (kj0 >= qi0 - BAND)

    def scores(q, k2):
        return lax.dot_general(q.astype(BF16), k2.astype(BF16), (((1,), (1,)), ((), ())),
                               preferred_element_type=F32)

    def process(cur, prev):
        if prev is None:
            k2, v2, mask = ks[cur, :], v_ref[cur, :], first_mask
        else:
            k2 = jnp.concatenate([ks[prev, :], ks[cur, :]], axis=0)
            v2 = jnp.concatenate([v_ref[prev, :], v_ref[cur, :]], axis=0)
            mask = band_mask
        s = jnp.where(mask, scores(qs[cur, :], k2), NEG_INF)
        m_old = ms[cur, :]
        m_new = jnp.maximum(m_old[:, :1], jnp.max(s, axis=1, keepdims=True))
        alpha = jnp.exp2(m_old - m_new)
        p = jnp.exp2(s - m_new)
        ls[cur, :] = alpha * ls[cur, :] + jnp.sum(p, axis=1, keepdims=True)
        acc[cur, :] = alpha * acc[cur, :] + jnp.dot(p.astype(BF16), v2.astype(BF16),
                                                    preferred_element_type=F32)
        ms[cur, :] = jnp.broadcast_to(m_new, (qb, HEAD_DIM))

    for d in DSWA_DILATIONS:
        n_blk = seq // (d * qb)
        step = d * qb
        for r in range(d):
            def rows_of(start, size):
                return pl.ds(start, size) if d == 1 else pl.ds(start, size, stride=d)

            def body(n, carry):
                first = r + n * step
                process(rows_of(first, qb), rows_of(first - d * BAND, BAND))
                return carry

            process(rows_of(r, qb), None)
            lax.fori_loop(1, n_blk, body, 0, unroll=2)

    def fin(i, carry):
        sl = pl.ds(pl.multiple_of(i * rows, rows), rows)
        o_ref[sl, :] = (acc[sl, :] / ls[sl, :]).astype(o_ref.dtype)
        return carry

    lax.fori_loop(0, seq // rows, fin, 0)


def dilated_mixture_attention(h, tables, *, q_col, k_col, v_col, n_heads):
    b, seq, _ = h.shape
    assert seq % (max(DSWA_DILATIONS) * DSWA_QUERY_BLOCK) == 0
    col = lambda c0: pl.BlockSpec((None, seq, HEAD_DIM), lambda bi, hi: (bi, 0, c0 + hi))
    tab = pl.BlockSpec((seq, HEAD_DIM), lambda bi, hi: (0, 0))
    vm = lambda: pltpu.VMEM((seq, HEAD_DIM), F32)
    return pl.pallas_call(
        functools.partial(_dswa_kernel, seq=seq),
        out_shape=jax.ShapeDtypeStruct((b, seq, n_heads * HEAD_DIM), BF16),
        grid=(b, n_heads),
        in_specs=[col(q_col), col(k_col), col(v_col), tab, tab, tab],
        out_specs=pl.BlockSpec((None, seq, HEAD_DIM), lambda bi, hi: (bi, 0, hi)),
        scratch_shapes=[vm(), vm(), vm(), vm(), vm()],
        compiler_params=_params("arbitrary", "arbitrary"),
        name="dswa",
    )(h, h, h, *tables)


def _nt(a, b):
    return lax.dot_general(a.astype(BF16), b.astype(BF16), (((1,), (1,)), ((), ())),
                           preferred_element_type=F32)


def _nn(a, b):
    return jnp.dot(a.astype(BF16), b.astype(BF16), preferred_element_type=F32)


def _gdn_kernel(alog_ref, dtb_ref, q_ref, k_ref, v_ref, gate_ref, cq_ref, ck_ref, cv_ref, ab_ref,
                nw_ref, o_ref, qn, kn, vn, gcs, gct, bts, us, wq, ak, *, seq):
    hi = pl.program_id(1)
    c = GDN_CHUNK
    rows = 2 * c
    halo = 8
    neg_a = -jnp.exp(jnp.full((1, 1), alog_ref[hi], F32))
    dt_b = dtb_ref[hi]
    row_i = lax.broadcasted_iota(jnp.int32, (c, c), 0)
    col_i = lax.broadcasted_iota(jnp.int32, (c, c), 1)
    tri_incl = (col_i <= row_i).astype(F32)
    eye = (col_i == row_i).astype(F32)
    lane32 = lax.broadcasted_iota(jnp.int32, (rows, 2 * GDN_HEADS), 1)

    def conv_silu(xh, w_ref):
        y = xh[halo:, :] * w_ref[GDN_CONV - 1:GDN_CONV, :]
        for j in range(1, GDN_CONV):
            y = y + pltpu.roll(xh, j, 0)[halo:, :] * w_ref[GDN_CONV - 1 - j:GDN_CONV - j, :]
        return y * (1.0 / (1.0 + jnp.exp(-y)))

    def l2n(t):
        return t * lax.rsqrt(jnp.sum(t * t, axis=-1, keepdims=True) + NORM_EPS)

    def prep(i, first):
        r0 = pl.multiple_of(i * rows, rows)
        sl = pl.ds(r0, rows)

        def with_halo(ref):
            if first:
                return jnp.concatenate([jnp.zeros((halo, HEAD_DIM), F32), ref[sl, :]], axis=0)
            return ref[pl.ds(pl.multiple_of(r0 - halo, halo), rows + halo), :]

        qn[sl, :] = l2n(conv_silu(with_halo(q_ref), cq_ref)) * (HEAD_DIM ** -0.5)
        kn[sl, :] = l2n(conv_silu(with_halo(k_ref), ck_ref))
        vn[sl, :] = conv_silu(with_halo(v_ref), cv_ref)
        ab = ab_ref[sl, :]
        b_logit = jnp.sum(jnp.where(lane32 == hi, ab, 0.0), axis=1, keepdims=True)
        a_logit = jnp.sum(jnp.where(lane32 == hi + GDN_HEADS, ab, 0.0), axis=1, keepdims=True)
        z = a_logit + dt_b
        g = neg_a * (jnp.maximum(z, 0.0) + jnp.log(1.0 + jnp.exp(-jnp.abs(z))))
        bts[sl, :] = jnp.broadcast_to(1.0 / (1.0 + jnp.exp(-b_logit)), (rows, HEAD_DIM))
        gb = jnp.broadcast_to(g, (rows, HEAD_DIM))
        for half in range(rows // c):
            hs = pl.ds(pl.multiple_of(r0 + half * c, c), c)
            gc = jnp.dot(tri_incl, gb[half * c:(half + 1) * c, :], precision=lax.Precision.HIGHEST,
                         preferred_element_type=F32)
            gcs[hs, :] = gc
            gct[hs, :] = gc.T

    prep(0, True)
    lax.fori_loop(1, seq // rows, lambda i, carry: (prep(i, False), carry)[1], 0)

    hd = HEAD_DIM
    left = lax.broadcasted_iota(jnp.int32, (c, 2 * c), 1) < c
    causal2 = jnp.concatenate([col_i <= row_i, col_i <= row_i], axis=1)
    strict2 = jnp.concatenate([col_i < row_i, col_i < row_i], axis=1)
    eye2 = jnp.concatenate([eye, eye], axis=1)

    def pair_diag(x2):
        return jnp.concatenate([jnp.where(left, x2, 0.0), jnp.where(left, 0.0, x2)], axis=0)

    def independent(qa, ka, va, ba, gca, gra, qb, kb_, vb, bb, gcb, grb):
        lanes = lambda xa, xb: jnp.concatenate([xa, xb], axis=1)
        q2, k2, beta2, gc2 = lanes(qa, qb), lanes(ka, kb_), lanes(ba, bb), lanes(gca, gcb)
        decay2 = jnp.exp(jnp.where(causal2, gc2 - lanes(gra, grb), -jnp.inf))
        kbeta2 = k2 * beta2
        kq = _nt(jnp.concatenate([kbeta2, q2], axis=0), pair_diag(k2))
        nm = kq[:c, :] * jnp.where(strict2, decay2, 0.0)
        attn2 = kq[c:, :] * decay2
        inv = eye2 - nm
        pw = _nn(nm, pair_diag(nm))
        for lvl in range(6):
            if lvl < 5:
                both = _nn(jnp.concatenate([inv, pw], axis=0), pair_diag(pw))
                inv, pw = inv + both[:c, :], both[c:, :]
            else:
                inv = inv + _nn(inv, pair_diag(pw))
        egc2 = jnp.exp(gc2)
        vbeta2, kbe2, qd2 = lanes(va, vb) * beta2, kbeta2 * egc2, q2 * egc2
        rhs_a = jnp.concatenate([vbeta2[:, :hd], kbe2[:, :hd]], axis=1)
        rhs_b = jnp.concatenate([vbeta2[:, hd:], kbe2[:, hd:]], axis=1)
        zero = jnp.zeros_like(rhs_a)
        uw = _nn(inv, jnp.concatenate([jnp.concatenate([rhs_a, zero], axis=1),
                                       jnp.concatenate([zero, rhs_b], axis=1)], axis=0))
        out = []
        for h, (k, gc) in enumerate(((ka, gca), (kb_, gcb))):
            lo = h * 2 * hd
            kdt = (k * jnp.exp(gc[c - 1:c, :] - gc)).T
            out.append((uw[:, lo:lo + hd],
                        jnp.concatenate([uw[:, lo + hd:lo + 2 * hd], qd2[:, h * hd:(h + 1) * hd]],
                                        axis=0).astype(BF16),
                        jnp.concatenate([attn2[:, h * hd:(h + 1) * hd], kdt], axis=0).astype(BF16)))
        return out

    group = GDN_INTERLEAVE
    assert group % 2 == 0

    def independent_group(i, carry):
        sls = [pl.ds(pl.multiple_of((i * group + u) * c, c), c) for u in range(group)]
        s2s = [pl.ds(pl.multiple_of((i * group + u) * (2 * c), 2 * c), 2 * c) for u in range(group)]
        loaded = [tuple(ref[sl, :] for ref in (qn, kn, vn, bts, gcs, gct)) for sl in sls]
        results = []
        for u in range(0, group, 2):
            results += independent(*loaded[u], *loaded[u + 1])
        for sl, s2, (u_new, wq_new, ak_new) in zip(sls, s2s, results):
            us[sl, :] = u_new
            wq[s2, :] = wq_new
            ak[s2, :] = ak_new
        return carry

    lax.fori_loop(0, seq // (c * group), independent_group, 0)

    def scan(n, state):
        sl = pl.ds(pl.multiple_of(n * c, c), c)
        s2 = pl.ds(pl.multiple_of(n * (2 * c), 2 * c), 2 * c)
        ws_qs = _nn(wq[s2, :], state)
        v_new = us[sl, :] - ws_qs[:c, :]
        av_kv = _nn(ak[s2, :], v_new)
        o = ws_qs[c:, :] + av_kv[:c, :]
        last = jnp.exp(gcs[pl.ds(n * c + c - 1, 1), :])
        o = o * lax.rsqrt(jnp.mean(o * o, axis=-1, keepdims=True) + NORM_EPS) * nw_ref[...]
        gate = gate_ref[sl, :]
        o_ref[sl, :] = (o * (gate * (1.0 / (1.0 + jnp.exp(-gate))))).astype(o_ref.dtype)
        return state * last + av_kv[c:, :]

    lax.fori_loop(0, seq // c, scan, jnp.zeros((HEAD_DIM, HEAD_DIM), F32))


def gated_delta_net(h, ab, conv_w, a_log, dt_bias, norm_w, *, q_col, k_col, v_col, gate_col):
    b, seq, _ = h.shape
    nh = GDN_HEADS
    assert seq % (max(2, GDN_INTERLEAVE) * GDN_CHUNK) == 0
    col = lambda c0: pl.BlockSpec((None, seq, HEAD_DIM), lambda bi, hi, *_: (bi, 0, c0 + hi))
    cw = lambda c0: pl.BlockSpec((GDN_CONV, HEAD_DIM), lambda bi, hi, *_: (0, c0 + hi))
    f32s = lambda: pltpu.VMEM((seq, HEAD_DIM), F32)
    b16s = lambda: pltpu.VMEM((2 * seq, HEAD_DIM), BF16)
    grid_spec = pltpu.PrefetchScalarGridSpec(
        num_scalar_prefetch=2,
        grid=(b, nh),
        in_specs=[col(q_col), col(k_col), col(v_col), col(gate_col), cw(0), cw(nh), cw(2 * nh),
                  pl.BlockSpec((None, seq, 2 * nh), lambda bi, hi, *_: (bi, 0, 0)),
                  pl.BlockSpec((1, HEAD_DIM), lambda bi, hi, *_: (0, 0))],
        out_specs=pl.BlockSpec((None, seq, HEAD_DIM), lambda bi, hi, *_: (bi, 0, hi)),
        scratch_shapes=[f32s(), f32s(), f32s(), f32s(), f32s(), f32s(), f32s(), b16s(), b16s()],
    )
    return pl.pallas_call(
        functools.partial(_gdn_kernel, seq=seq),
        out_shape=jax.ShapeDtypeStruct((b, seq, nh * HEAD_DIM), BF16),
        grid_spec=grid_spec,
        compiler_params=_params("arbitrary", "arbitrary"),
        name="gdn",
    )(a_log, dt_bias, h, h, h, h, conv_w, conv_w, conv_w, ab, norm_w.reshape(1, HEAD_DIM))


def _fox_cum_kernel(fl_ref, bf_ref, o_ref, *, seq):
    c = 128
    tri_incl = (lax.broadcasted_iota(jnp.int32, (c, c), 1)
                <= lax.broadcasted_iota(jnp.int32, (c, c), 0)).astype(F32)

    def body(n, carry):
        sl = pl.ds(pl.multiple_of(n * c, c), c)
        z = fl_ref[sl, :] + bf_ref[...]
        log_f = -(jnp.maximum(-z, 0.0) + jnp.log(1.0 + jnp.exp(-jnp.abs(z))))
        cs = jnp.dot(tri_incl, log_f, precision=lax.Precision.HIGHEST,
                     preferred_element_type=F32) + carry
        o_ref[sl, :] = cs * LOG2E
        return cs[c - 1:c, :]

    lax.fori_loop(0, seq // c, body, jnp.zeros((1, fl_ref.shape[-1]), F32))


def forget_gate_cumsum(f_logit, b_f):
    b, seq, nh = f_logit.shape
    blk = pl.BlockSpec((None, seq, nh), lambda bi: (bi, 0, 0))
    return pl.pallas_call(
        functools.partial(_fox_cum_kernel, seq=seq),
        out_shape=jax.ShapeDtypeStruct((b, seq, nh), F32),
        grid=(b,),
        in_specs=[blk, pl.BlockSpec((1, nh), lambda bi: (0, 0))],
        out_specs=blk,
        compiler_params=_params("arbitrary"),
        name="fox_cumsum",
    )(f_logit, b_f.reshape(1, nh))


def _fox_kernel(q_ref, k_ref, v_ref, cum_ref, o_ref, *, blk, heads):
    qi = pl.program_id(2)
    hd = HEAD_DIM
    ri = lax.broadcasted_iota(jnp.int32, (blk, blk), 0)
    ci = lax.broadcasted_iota(jnp.int32, (blk, blk), 1)
    eye = (lax.broadcasted_iota(jnp.int32, (hd, hd), 0)
           == lax.broadcasted_iota(jnp.int32, (hd, hd), 1))
    qs = pl.ds(pl.multiple_of(qi * blk, blk), blk)

    def as_column(row):
        return jnp.concatenate(
            [jnp.sum(jnp.where(eye, row[:, c * hd:(c + 1) * hd], 0.0), axis=1, keepdims=True)
             for c in range(blk // hd)], axis=0)

    cq = [as_column(cum_ref[g:g + 1, qs]) for g in range(heads)]
    q = [q_ref[:, g * hd:(g + 1) * hd] for g in range(heads)]

    def step(j, carry, diagonal):
        ks = pl.ds(pl.multiple_of(j * blk, blk), blk)
        out = []
        for g in range(heads):
            m, l, acc = carry[g]
            s = _nt(q[g], k_ref[ks, g * hd:(g + 1) * hd]) + (cq[g] - cum_ref[g:g + 1, ks])
            if diagonal:
                s = jnp.where(ri >= ci, s, NEG_INF)
            m_new = jnp.maximum(m, jnp.max(s, axis=1, keepdims=True))
            alpha = jnp.exp2(m - m_new)
            p = jnp.exp2(s - m_new)
            l = alpha * l + jnp.sum(p, axis=1, keepdims=True)
            acc = alpha * acc + jnp.dot(p.astype(BF16), v_ref[ks, g * hd:(g + 1) * hd],
                                        preferred_element_type=F32)
            out.append((m_new, l, acc))
        return tuple(out)

    init = tuple((jnp.full((blk, 1), NEG_INF, F32), jnp.zeros((blk, 1), F32),
                  jnp.zeros((blk, hd), F32)) for _ in range(heads))
    carry = lax.fori_loop(0, qi, lambda j, cr: step(j, cr, False), init)
    carry = step(qi, carry, True)
    for g in range(heads):
        _, l, acc = carry[g]
        o_ref[:, g * hd:(g + 1) * hd] = (acc / l).astype(o_ref.dtype)


def forgetting_attention(hq, cum_rows, *, n_heads, blk=512, heads=2):
    b, seq, _ = hq.shape
    assert seq % blk == 0 and n_heads % heads == 0
    w = heads * HEAD_DIM
    ng = n_heads // heads
    kv = lambda c0: pl.BlockSpec((None, seq, w), lambda bi, hi, qi: (bi, 0, c0 + hi))
    return pl.pallas_call(
        functools.partial(_fox_kernel, blk=blk, heads=heads),
        out_shape=jax.ShapeDtypeStruct((b, seq, n_heads * HEAD_DIM), BF16),
        grid=(b, ng, seq // blk),
        in_specs=[pl.BlockSpec((None, blk, w), lambda bi, hi, qi: (bi, qi, hi)),
                  kv(ng), kv(2 * ng),
                  pl.BlockSpec((None, None, heads, seq), lambda bi, hi, qi: (bi, hi, 0, 0))],
        out_specs=pl.BlockSpec((None, blk, w), lambda bi, hi, qi: (bi, qi, hi)),
        compiler_params=_params("arbitrary", "arbitrary", "arbitrary"),
        name="fox",
    )(hq, hq, hq, cum_rows)


def _router_kernel(x_ref, rw_ref, rb_ref, idx_ref, gate_ref):
    logits = lax.dot_general(rw_ref[...], x_ref[...], (((1,), (1,)), ((), ())),
                             precision=lax.Precision.HIGHEST, preferred_element_type=F32)
    score = 1.0 / (1.0 + jnp.exp(-logits))
    sel = score + rb_ref[...]
    sel_r = [sel[e:e + 1, :] for e in range(N_EXPERTS)]
    score_r = [score[e:e + 1, :] for e in range(N_EXPERTS)]

    def top2_sum(a, b, c, d):
        hi1, lo1, hi2, lo2 = jnp.maximum(a, b), jnp.minimum(a, b), jnp.maximum(c, d), jnp.minimum(c, d)
        return jnp.maximum(hi1, hi2) + jnp.maximum(jnp.minimum(hi1, hi2), jnp.maximum(lo1, lo2))

    epg = EXPERTS_PER_GROUP
    best = top2_sum(*sel_r[:epg])
    grp = jnp.zeros_like(best, dtype=jnp.int32)
    for g in range(1, N_GROUPS):
        cand = top2_sum(*sel_r[g * epg:(g + 1) * epg])
        take = cand > best
        best = jnp.where(take, cand, best)
        grp = jnp.where(take, g, grp)
    in_sel, in_score = [], []
    for j in range(epg):
        s_j, u_j = sel_r[j], score_r[j]
        for g in range(1, N_GROUPS):
            s_j = jnp.where(grp == g, sel_r[g * epg + j], s_j)
            u_j = jnp.where(grp == g, score_r[g * epg + j], u_j)
        in_sel.append(s_j)
        in_score.append(u_j)

    def arg_best(vals):
        bv, bi = vals[0], jnp.zeros_like(grp)
        for j in range(1, epg):
            take = vals[j] > bv
            bv = jnp.where(take, vals[j], bv)
            bi = jnp.where(take, j, bi)
        return bi

    i1 = arg_best(in_sel)
    i2 = arg_best([jnp.where(i1 == j, -jnp.inf, in_sel[j]) for j in range(epg)])

    def pick(vals, i):
        out = vals[0]
        for j in range(1, epg):
            out = jnp.where(i == j, vals[j], out)
        return out

    g1, g2 = pick(in_score, i1), pick(in_score, i2)
    tot = g1 + g2
    idx_ref[0:1, :] = grp * epg + i1
    idx_ref[1:2, :] = grp * epg + i2
    gate_ref[0:1, :] = g1 / tot
    gate_ref[1:2, :] = g2 / tot


def route(x, router_w, router_bias, *, tm=512):
    t, d = x.shape
    tm = min(tm, t)
    out = pl.BlockSpec((2, tm), lambda i: (0, i))
    return pl.pallas_call(
        _router_kernel,
        out_shape=(jax.ShapeDtypeStruct((2, t), jnp.int32), jax.ShapeDtypeStruct((2, t), F32)),
        grid=(t // tm,),
        in_specs=[pl.BlockSpec((tm, d), lambda i: (i, 0)),
                  pl.BlockSpec((N_EXPERTS, d), lambda i: (0, 0)),
                  pl.BlockSpec((N_EXPERTS, 1), lambda i: (0, 0))],
        out_specs=(out, out),
        compiler_params=_params("arbitrary"),
        name="router",
    )(x, router_w.T, router_bias.reshape(N_EXPERTS, 1))


def _moe_up_kernel(be_ref, first_ref, nused_ref, x_ref, wg_ref, wu_ref, o_ref, wgb, wub):
    i = pl.program_id(1)

    @pl.when(i < nused_ref[0])
    def _():
        @pl.when(first_ref[i] == 1)
        def _():
            wgb[...] = wg_ref[...].astype(BF16)
            wub[...] = wu_ref[...].astype(BF16)

        xs = x_ref[...]
        a = jnp.dot(xs, wgb[...], preferred_element_type=F32)
        u = jnp.dot(xs, wub[...], preferred_element_type=F32)
        o_ref[...] = (a * (1.0 / (1.0 + jnp.exp(-a))) * u).astype(o_ref.dtype)

    @pl.when(i >= nused_ref[0])
    def _():
        o_ref[...] = jnp.zeros_like(o_ref)


def _moe_down_kernel(be_ref, nused_ref, h_ref, wd_ref, o_ref, *, rows, tn, d_chunks):
    i, j = pl.program_id(0), pl.program_id(1)
    n_chunks = tn // V7X_LANES

    def store(y):
        for cc in range(n_chunks):
            o_ref[pl.ds(j * n_chunks + cc, rows, stride=ROW_STRIDE), :] = (
                y[:, cc * V7X_LANES:(cc + 1) * V7X_LANES])

    @pl.when(i < nused_ref[0])
    def _():
        store(jnp.dot(h_ref[...], wd_ref[...].astype(BF16), preferred_element_type=F32))

    @pl.when(i >= nused_ref[0])
    def _():
        store(jnp.zeros((rows, tn), F32))

    @pl.when(j == 0)
    def _():
        for c in range(d_chunks, ROW_STRIDE):
            o_ref[pl.ds(c, rows, stride=ROW_STRIDE), :] = jnp.zeros((rows, V7X_LANES), F32)


def expert_ffn(xg, block_expert, block_first, n_used, w_gate, w_up, w_down, layer, *, tn_up=384,
               tn_down=2048):
    r, d = xg.shape
    de = w_gate.shape[-1]
    tn_up, tn_down = min(tn_up, de), min(tn_down, d)
    nb = r // MOE_ROWS
    w_in = pl.BlockSpec((None, None, d, tn_up), lambda j, i, be, fi, nu: (layer, be[i], 0, j))
    hid = pl.pallas_call(
        _moe_up_kernel,
        out_shape=jax.ShapeDtypeStruct((r, de), BF16),
        grid_spec=pltpu.PrefetchScalarGridSpec(
            num_scalar_prefetch=3,
            grid=(de // tn_up, nb),
            in_specs=[pl.BlockSpec((MOE_ROWS, d), lambda j, i, *_: (i, 0)), w_in, w_in],
            out_specs=pl.BlockSpec((MOE_ROWS, tn_up), lambda j, i, *_: (i, j)),
            scratch_shapes=[pltpu.VMEM((d, tn_up), BF16), pltpu.VMEM((d, tn_up), BF16)],
        ),
        compiler_params=_params("arbitrary", "arbitrary"),
        name="moe_up",
    )(block_expert, block_first, n_used, xg, w_gate, w_up)
    d_chunks = d // V7X_LANES
    assert d_chunks <= ROW_STRIDE
    return pl.pallas_call(
        functools.partial(_moe_down_kernel, rows=MOE_ROWS, tn=tn_down, d_chunks=d_chunks),
        out_shape=jax.ShapeDtypeStruct((r * ROW_STRIDE, V7X_LANES), F32),
        grid_spec=pltpu.PrefetchScalarGridSpec(
            num_scalar_prefetch=2,
            grid=(nb, d // tn_down),
            in_specs=[pl.BlockSpec((MOE_ROWS, de), lambda i, j, *_: (i, 0)),
                      pl.BlockSpec((None, None, de, tn_down),
                                   lambda i, j, be, nu: (layer, be[i], 0, j))],
            out_specs=pl.BlockSpec((MOE_ROWS * ROW_STRIDE, V7X_LANES), lambda i, j, *_: (i, 0)),
        ),
        compiler_params=_params("arbitrary", "arbitrary"),
        name="moe_down",
    )(block_expert, n_used, hid, w_down)


DMA_PRIORITIES = 2
DMA_BURST = 8


def _token_row_copy(src_ref, src_tok, dst_ref, dst_tok, n_chunks, sem):
    src = pl.ds(pl.multiple_of(src_tok * ROW_STRIDE, ROW_STRIDE), n_chunks)
    dst = pl.ds(pl.multiple_of(dst_tok * ROW_STRIDE, ROW_STRIDE), n_chunks)
    return pltpu.make_async_copy(src_ref.at[src], dst_ref.at[dst], sem)


def _start_token_gather(idx_of, src_ref, buf_ref, n, n_chunks, sem):
    assert n % DMA_BURST == 0

    def issue(g, carry):
        for u in range(DMA_BURST):
            r = g * DMA_BURST + u
            _token_row_copy(src_ref, idx_of(r), buf_ref, r, n_chunks, sem).start(
                priority=u % DMA_PRIORITIES)
        return carry

    lax.fori_loop(0, n // DMA_BURST, issue, 0)


def _wait_token_gather(src_ref, buf_ref, n, n_chunks, sem):
    def drain(g, carry):
        for u in range(DMA_BURST):
            _token_row_copy(src_ref, 0, buf_ref, g * DMA_BURST + u, n_chunks, sem).wait()
        return carry

    lax.fori_loop(0, n // DMA_BURST, drain, 0)


def _gather_kernel(idx_ref, nused_ref, src_ref, o_ref, buf, sems, *, rows):
    i = pl.program_id(0)
    n_used = nused_ref[0]
    n_chunks = o_ref.shape[1] // V7X_LANES
    slot = i % 2

    def start(block, s):
        _start_token_gather(lambda r: idx_ref[block * rows + r], src_ref, buf.at[s], rows, n_chunks,
                            sems.at[s])

    @pl.when((i == 0) & (n_used > 0))
    def _():
        start(0, 0)

    @pl.when(i + 1 < n_used)
    def _():
        start(i + 1, 1 - slot)

    @pl.when(i < n_used)
    def _():
        _wait_token_gather(src_ref, buf.at[slot], rows, n_chunks, sems.at[slot])
        for c in range(n_chunks):
            o_ref[:, c * V7X_LANES:(c + 1) * V7X_LANES] = _token_chunk(buf.at[slot], c, rows).astype(BF16)

    @pl.when(i >= n_used)
    def _():
        o_ref[...] = jnp.zeros_like(o_ref)


def gather_rows(src_rows, idx, n_used, d, *, rows=MOE_ROWS):
    m = idx.shape[0]
    assert m % rows == 0
    return pl.pallas_call(
        functools.partial(_gather_kernel, rows=rows),
        out_shape=jax.ShapeDtypeStruct((m, d), BF16),
        grid_spec=pltpu.PrefetchScalarGridSpec(
            num_scalar_prefetch=2,
            grid=(m // rows,),
            in_specs=[pl.BlockSpec(memory_space=pl.ANY)],
            out_specs=pl.BlockSpec((rows, d), lambda i, *_: (i, 0)),
            scratch_shapes=[pltpu.VMEM((2, rows * ROW_STRIDE, V7X_LANES), F32),
                            pltpu.SemaphoreType.DMA((2,))],
        ),
        compiler_params=_params("arbitrary"),
        name="gather_rows",
    )(idx, n_used, src_rows)


def _combine_ln_kernel(pos_ref, x_ref, y_ref, gate_ref, g_ref, b_ref, o_ref, ob_ref, ya, yb, sems,
                       *, tm, n_tok):
    i = pl.program_id(0)
    n_chunks = x_ref.shape[1] // V7X_LANES
    slot = i % 2

    def start(tile, s):
        _start_token_gather(lambda r: pos_ref[tile * tm + r], y_ref, ya.at[s], tm, n_chunks, sems.at[s])
        _start_token_gather(lambda r: pos_ref[n_tok + tile * tm + r], y_ref, yb.at[s], tm, n_chunks,
                            sems.at[s])

    @pl.when(i == 0)
    def _():
        start(0, 0)

    @pl.when(i + 1 < pl.num_programs(0))
    def _():
        start(i + 1, 1 - slot)

    _wait_token_gather(y_ref, ya.at[slot], tm, n_chunks, sems.at[slot])
    _wait_token_gather(y_ref, yb.at[slot], tm, n_chunks, sems.at[slot])
    g0, g1 = gate_ref[:, 0:1], gate_ref[:, 1:2]
    ffn = jnp.concatenate([_token_chunk(ya.at[slot], c, tm) * g0 + _token_chunk(yb.at[slot], c, tm) * g1
                           for c in range(n_chunks)], axis=1)
    out = _layer_norm_rows(DEEPNORM_ALPHA * x_ref[...] + ffn, g_ref, b_ref)
    o_ref[...] = out
    ob_ref[...] = out.astype(BF16)


def moe_combine_layer_norm(x, y_rows, pos, gate, g, b, layer, *, tm=256):
    t, d = x.shape
    tm = min(tm, t)
    row = pl.BlockSpec((tm, d), lambda i, *_: (i, 0))
    par = pl.BlockSpec((None, 1, d), lambda i, *_: (layer, 0, 0))
    buf = lambda: pltpu.VMEM((2, tm * ROW_STRIDE, V7X_LANES), F32)
    return pl.pallas_call(
        functools.partial(_combine_ln_kernel, tm=tm, n_tok=t),
        out_shape=(jax.ShapeDtypeStruct((t, d), F32), jax.ShapeDtypeStruct((t, d), BF16)),
        grid_spec=pltpu.PrefetchScalarGridSpec(
            num_scalar_prefetch=1,
            grid=(t // tm,),
            in_specs=[row, pl.BlockSpec(memory_space=pl.ANY),
                      pl.BlockSpec((tm, 2), lambda i, *_: (i, 0)), par, par],
            out_specs=(row, row),
            scratch_shapes=[buf(), buf(), pltpu.SemaphoreType.DMA((2,))],
        ),
        compiler_params=_params("arbitrary"),
        name="moe_combine_ln",
    )(pos, x, y_rows, gate, g.reshape(-1, 1, d), b.reshape(-1, 1, d))


def moe_ffn(x, x_rows, router_w, router_bias, w_gate, w_up, w_down, layer):
    t, d = x.shape
    idx, gate = route(x, router_w, router_bias)
    flat_e = idx.reshape(-1)
    n_assign = flat_e.shape[0]
    order = jnp.argsort(flat_e).astype(jnp.int32)
    e_sorted = flat_e[order]
    bounds = jnp.searchsorted(e_sorted, jnp.arange(N_EXPERTS + 1, dtype=jnp.int32)).astype(jnp.int32)
    start, counts = bounds[:-1], bounds[1:] - bounds[:-1]
    padded = (counts + MOE_ROWS - 1) // MOE_ROWS * MOE_ROWS
    pad_end = jnp.cumsum(padded)
    pad_start = pad_end - padded
    dest_sorted = pad_start[e_sorted] + jnp.arange(n_assign, dtype=jnp.int32) - start[e_sorted]
    n_blocks = -(-(n_assign + N_EXPERTS * (MOE_ROWS - 1)) // MOE_ROWS)
    rows = n_blocks * MOE_ROWS
    blk_start = jnp.arange(n_blocks, dtype=jnp.int32) * MOE_ROWS
    block_expert = jnp.minimum(jnp.searchsorted(pad_end, blk_start, side='right'),
                               N_EXPERTS - 1).astype(jnp.int32)
    n_used = (pad_end[-1] // MOE_ROWS).astype(jnp.int32).reshape(1)
    block_first = jnp.concatenate([jnp.ones((1,), jnp.int32),
                                   (block_expert[1:] != block_expert[:-1]).astype(jnp.int32)])
    tok_sorted = order % t
    row_tok = (jnp.arange(rows, dtype=jnp.int32) % t).at[dest_sorted].set(tok_sorted)
    pos = jnp.zeros((n_assign,), jnp.int32).at[order].set(dest_sorted)
    xg = gather_rows(x_rows, row_tok, n_used, d)
    y_rows = expert_ffn(xg, block_expert, block_first, n_used, w_gate, w_up, w_down, layer)
    return y_rows, pos, gate.T


def _q_scale(n_q, n_total):
    return jnp.concatenate([jnp.full((1, n_q), ATTN_Q_SCALE, F32),
                            jnp.ones((1, n_total - n_q), F32)], axis=1)


def kernel(x, w_in_even, conv_w, a_log, dt_bias, gdn_norm_w, w_out_even, w_in_odd, b_f, w_out_odd, ln_mix_g, ln_mix_b, ln_ffn_g, ln_ffn_b, w_gate, w_up, w_down, router_w, router_bias):
    b, seq, d = x.shape
    t = b * seq
    xf = x.reshape(t, d)
    xb = xf.astype(BF16)
    tables = rotary_tables(seq)
    hb = HEAD_DIM
    for layer in range(DEPTH):
        i = layer // 2
        if layer % 2 == 0:
            wa, wqkv, wg = DSWA_HEADS * hb, 3 * GDN_HEADS * hb, GDN_HEADS * hb
            n_main = 3 * wa + wqkv + wg
            w_in_t = jnp.swapaxes(w_in_even[i], 0, 1)
            h = matmul_nt(xb, w_in_t, nrows=n_main,
                          col_scale=_q_scale(wa, n_main)).reshape(b, seq, n_main)
            ab = matmul_nt(xb, w_in_t[n_main:]).reshape(b, seq, 2 * GDN_HEADS)
            o_a = dilated_mixture_attention(h, tables, q_col=0, k_col=DSWA_HEADS,
                                            v_col=2 * DSWA_HEADS, n_heads=DSWA_HEADS)
            c0 = 3 * DSWA_HEADS
            o_b = gated_delta_net(h, ab, conv_w[i], a_log[i], dt_bias[i], gdn_norm_w[i],
                                  q_col=c0, k_col=c0 + GDN_HEADS, v_col=c0 + 2 * GDN_HEADS,
                                  gate_col=c0 + 3 * GDN_HEADS)
            o = jnp.concatenate([o_a, o_b], axis=-1).reshape(t, -1)
            mix = matmul(o, w_out_even, lead=i)
        else:
            n_main = 3 * FOX_HEADS * hb
            w_in_t = jnp.swapaxes(w_in_odd[i], 0, 1)
            hq = matmul_nt(xb, w_in_t, nrows=n_main, col_scale=_q_scale(FOX_HEADS * hb, n_main),
                           out_dtype=BF16).reshape(b, seq, n_main)
            f_logit = matmul_nt(xb, w_in_t[n_main:]).reshape(b, seq, FOX_HEADS)
            cum = forget_gate_cumsum(f_logit, b_f[i])
            pair = 2
            cum_rows = cum.transpose(0, 2, 1).reshape(b, FOX_HEADS // pair, pair, seq)
            o = forgetting_attention(hq, cum_rows, n_heads=FOX_HEADS, heads=pair).reshape(t, -1)
            mix = matmul(o, w_out_odd, lead=i)
        xf, x_rows = residual_layer_norm(xf, mix, ln_mix_g, ln_mix_b, layer)
        y_rows, pos, gate = moe_ffn(xf, x_rows, router_w, router_bias, w_gate, w_up, w_down, layer)
        xf, xb = moe_combine_layer_norm(xf, y_rows, pos, gate, ln_ffn_g, ln_ffn_b, layer)
    return xf.reshape(b, seq, d)
```
